```python
import jax, jax.numpy as jnp
from jax import lax
import numpy as np

D_MODEL = 1024
BATCH = 8
SEQ = 8192
DEPTH = 2

CHUNK = 64
EPS = 1e-6
HEAD_DIM = 64
H_A = D_MODEL // (2 * HEAD_DIM)
A_PREV_CHUNKS = 8
MAX_REL_DIST = 256
H_B = D_MODEL // (2 * HEAD_DIM)
H_B_KV = H_B // 4
B_WINDOW = 128
B_PREV_CHUNKS = B_WINDOW // CHUNK
ATTN_PROJ = 3 * H_A * HEAD_DIM + H_B * HEAD_DIM + 2 * H_B_KV * HEAD_DIM
D_INNER = 2 * D_MODEL
SSM_HEAD_DIM = 64
SSM_HEADS = D_INNER // SSM_HEAD_DIM
SSM_GROUPS = 4
SSM_STATE = 128
SSM_CONV = 4
SSD_CHUNK = 64
SSM_CONV_CH = D_INNER + 2 * SSM_GROUPS * SSM_STATE
SSM_PROJ = D_INNER + SSM_CONV_CH + SSM_HEADS
D_FF = ((8 * D_MODEL // 3 + 127) // 128) * 128
FFN_CONV = 3

kernel_name = "chunk_causal_hybrid_attn_ssd_convffn"


def rmsnorm(x, g):
    xf = x.astype(jnp.float32)
    y = xf * lax.rsqrt(jnp.mean(xf * xf, axis=-1, keepdims=True) + EPS)
    return (y * g.astype(jnp.float32)).astype(x.dtype)


def causal_dwconv(x, w, b):
    k = w.shape[0]
    y = lax.conv_general_dilated(
        x, w[:, None, :].astype(x.dtype), window_strides=(1,), padding=[(k - 1, 0)],
        dimension_numbers=("NWC", "WIO", "NWC"), feature_group_count=x.shape[-1])
    return y + b.astype(x.dtype)


def band_offsets(n_prev):
    band = (n_prev + 1) * CHUNK
    q_off = jnp.arange(CHUNK, dtype=jnp.int32)
    k_off = jnp.arange(band, dtype=jnp.int32) - n_prev * CHUNK
    return q_off[:, None] - k_off[None, :], k_off


def band_attention(q, k, v, n_prev, bias, sinks):
    b, s, hq, d = q.shape
    hkv = k.shape[2]
    grp = hq // hkv
    nc = s // CHUNK
    band = (n_prev + 1) * CHUNK
    pad = n_prev * CHUNK
    kp = jnp.pad(k, ((0, 0), (pad, 0), (0, 0), (0, 0)))
    vp = jnp.pad(v, ((0, 0), (pad, 0), (0, 0), (0, 0)))
    qc = jnp.moveaxis(q.reshape(b, nc, CHUNK, hkv, grp, d), 1, 0)
    _, k_off = band_offsets(n_prev)
    scale = d ** -0.5

    def one_chunk(args):
        c, qb = args
        start = c * CHUNK
        kb = lax.dynamic_slice_in_dim(kp, start, band, axis=1)
        vb = lax.dynamic_slice_in_dim(vp, start, band, axis=1)
        sc = jnp.einsum("bqkgd,bskd->bkgqs", qb, kb).astype(jnp.float32) * scale + bias
        valid = (start + k_off) >= 0
        sc = jnp.where(valid, sc, -jnp.inf)
        if sinks is None:
            p = jax.nn.softmax(sc, axis=-1)
        else:
            snk = sinks.astype(jnp.float32)[None, :, :, None, None]
            m = jnp.maximum(jnp.max(sc, axis=-1, keepdims=True), snk)
            e = jnp.exp(sc - m)
            p = e / (jnp.sum(e, axis=-1, keepdims=True) + jnp.exp(snk - m))
        return jnp.einsum("bkgqs,bskd->bqkgd", p.astype(vb.dtype), vb)

    out = lax.map(one_chunk, (jnp.arange(nc, dtype=jnp.int32), qc))
    return jnp.moveaxis(out, 0, 1).reshape(b, s, hq * d)


def attn_layer(h, w_in, w_out, relpos_table, q_norm_a, k_norm_a, q_norm_b, k_norm_b, sinks):
    b, s, _ = h.shape
    da, db, dkv = H_A * HEAD_DIM, H_B * HEAD_DIM, H_B_KV * HEAD_DIM
    cuts = [da, 2 * da, 3 * da, 3 * da + db, 3 * da + db + dkv]
    qa, ka, va, qb, kb, vb = jnp.split(h @ w_in, cuts, axis=-1)
    heads = lambda t, n: t.reshape(b, s, n, HEAD_DIM)
    qa = rmsnorm(heads(qa, H_A), q_norm_a)
    ka = rmsnorm(heads(ka, H_A), k_norm_a)
    rel_a, _ = band_offsets(A_PREV_CHUNKS)
    idx = jnp.clip(rel_a, -MAX_REL_DIST, MAX_REL_DIST) + MAX_REL_DIST
    bias_a = relpos_table.astype(jnp.float32)[:, idx][:, None]
    oa = band_attention(qa, ka, heads(va, H_A), A_PREV_CHUNKS, bias_a, None)
    qb = rmsnorm(heads(qb, H_B), q_norm_b)
    kb = rmsnorm(heads(kb, H_B_KV), k_norm_b)
    rel_b, _ = band_offsets(B_PREV_CHUNKS)
    slopes = 2.0 ** (-8.0 * jnp.arange(1, H_B + 1, dtype=jnp.float32) / H_B)
    bias_b = (-slopes[:, None, None] * jnp.abs(rel_b).astype(jnp.float32)).reshape(
        H_B_KV, H_B // H_B_KV, CHUNK, (B_PREV_CHUNKS + 1) * CHUNK)
    ob = band_attention(qb, kb, heads(vb, H_B_KV), B_PREV_CHUNKS, bias_b,
                        sinks.reshape(H_B_KV, H_B // H_B_KV))
    return jnp.concatenate([oa, ob], axis=-1) @ w_out


def ssd_scan(x, dt, a, bm, cm):
    b, s, h, p = x.shape
    g, n = bm.shape[2], bm.shape[3]
    r = h // g
    L = SSD_CHUNK
    nc = s // L

    def to_chunks(t):
        return jnp.moveaxis(t.reshape((b, nc, L) + t.shape[2:]), 1, 0)

    xc = to_chunks(x.reshape(b, s, g, r, p))
    dtc = to_chunks(dt.reshape(b, s, g, r))
    bc, cc = to_chunks(bm), to_chunks(cm)
    a = a.reshape(g, r)
    causal = jnp.tril(jnp.ones((L, L), dtype=bool))[None, :, :, None, None]

    def step(state, inp):
        xk, dtk, bk, ck = inp
        acs = jnp.cumsum(dtk * a, axis=1)
        seg = acs[:, :, None] - acs[:, None, :]
        decay = jnp.exp(jnp.where(causal, seg, -jnp.inf))
        cb = jnp.einsum("blgn,bsgn->bgls", ck, bk)
        y_intra = jnp.einsum("bgls,blsgr,bsgrp->blgrp", cb, decay, xk * dtk[..., None])
        y_state = jnp.einsum("blgn,bgrpn->blgrp", ck, state) * jnp.exp(acs)[..., None]
        last = acs[:, -1]
        w_in = jnp.exp(last[:, None] - acs) * dtk
        new_state = state * jnp.exp(last)[..., None, None] + jnp.einsum(
            "bsgn,bsgr,bsgrp->bgrpn", bk, w_in, xk)
        return new_state, y_intra + y_state

    state0 = jnp.zeros((b, g, r, p, n), jnp.float32)
    _, ys = lax.scan(step, state0, (xc, dtc, bc, cc))
    return jnp.moveaxis(ys, 0, 1).reshape(b, s, h, p)


def ssm_layer(h, w_in, conv_w, conv_b, dt_bias, a_log, d_skip, norm_w, w_out):
    b, s, _ = h.shape
    z, xbc, dt = jnp.split(h @ w_in, [D_INNER, D_INNER + SSM_CONV_CH], axis=-1)
    xbc = jax.nn.silu(causal_dwconv(xbc, conv_w, conv_b))
    xs, bm, cm = jnp.split(xbc, [D_INNER, D_INNER + SSM_GROUPS * SSM_STATE], axis=-1)
    xs = xs.reshape(b, s, SSM_HEADS, SSM_HEAD_DIM).astype(jnp.float32)
    bm = bm.reshape(b, s, SSM_GROUPS, SSM_STATE).astype(jnp.float32)
    cm = cm.reshape(b, s, SSM_GROUPS, SSM_STATE).astype(jnp.float32)
    dt = jax.nn.softplus(dt.astype(jnp.float32) + dt_bias.astype(jnp.float32))
    a = -jnp.exp(a_log.astype(jnp.float32))
    y = ssd_scan(xs, dt, a, bm, cm) + d_skip.astype(jnp.float32)[:, None] * xs
    y = y.reshape(b, s, D_INNER) * jax.nn.silu(z.astype(jnp.float32))
    yg = y.reshape(b, s, SSM_GROUPS, D_INNER // SSM_GROUPS)
    yg = yg * lax.rsqrt(jnp.mean(yg * yg, axis=-1, keepdims=True) + EPS)
    y = (yg.reshape(b, s, D_INNER) * norm_w.astype(jnp.float32)).astype(h.dtype)
    return y @ w_out


def conv_ffn(h, w_in, conv_w, conv_b, w_out):
    gate, up = jnp.split(h @ w_in, [D_FF], axis=-1)
    gate = causal_dwconv(gate, conv_w, conv_b)
    return (jax.nn.silu(gate) * up) @ w_out


def setup_inputs(seed: int = 0) -> dict:
    key = jax.random.key(seed)
    ks = jax.random.split(key, 24)
    n_even, n_odd = (DEPTH + 1) // 2, DEPTH // 2
    nrm = lambda k, shape, scale: jax.random.normal(k, shape, jnp.float32) * scale
    dt0 = jnp.exp(jax.random.uniform(ks[15], (n_odd, SSM_HEADS), jnp.float32,
                                     np.log(1e-3), np.log(1e-1)))
    return {
        "x": nrm(ks[0], (BATCH, SEQ, D_MODEL), 1.0),
        "norm_mix": 1.0 + nrm(ks[1], (DEPTH, D_MODEL), 0.05),
        "norm_ffn": 1.0 + nrm(ks[2], (DEPTH, D_MODEL), 0.05),
        "attn_w_in": nrm(ks[3], (n_even, D_MODEL, ATTN_PROJ), D_MODEL ** -0.5),
        "attn_w_out": nrm(ks[4], (n_even, (H_A + H_B) * HEAD_DIM, D_MODEL), ((H_A + H_B) * HEAD_DIM) ** -0.5),
        "relpos_table": nrm(ks[5], (n_even, H_A, 2 * MAX_REL_DIST + 1), 0.2),
        "q_norm_a": 1.0 + nrm(ks[6], (n_even, HEAD_DIM), 0.05),
        "k_norm_a": 1.0 + nrm(ks[7], (n_even, HEAD_DIM), 0.05),
        "q_norm_b": 1.0 + nrm(ks[8], (n_even, HEAD_DIM), 0.05),
        "k_norm_b": 1.0 + nrm(ks[9], (n_even, HEAD_DIM), 0.05),
        "sinks": nrm(ks[10], (n_even, H_B), 0.5),
        "ssm_w_in": nrm(ks[11], (n_odd, D_MODEL, SSM_PROJ), D_MODEL ** -0.5),
        "ssm_conv_w": nrm(ks[12], (n_odd, SSM_CONV, SSM_CONV_CH), SSM_CONV ** -0.5),
        "ssm_conv_b": nrm(ks[13], (n_odd, SSM_CONV_CH), 0.02),
        "ssm_dt_bias": dt0 + jnp.log(-jnp.expm1(-dt0)),
        "ssm_a_log": jnp.log(jax.random.uniform(ks[14], (n_odd, SSM_HEADS), jnp.float32, 1.0, 16.0)),
        "ssm_d": 1.0 + nrm(ks[16], (n_odd, SSM_HEADS), 0.1),
        "ssm_norm": 1.0 + nrm(ks[17], (n_odd, D_INNER), 0.05),
        "ssm_w_out": nrm(ks[18], (n_odd, D_INNER, D_MODEL), D_INNER ** -0.5),
        "ffn_w_in": nrm(ks[19], (DEPTH, D_MODEL, 2 * D_FF), D_MODEL ** -0.5),
        "ffn_conv_w": nrm(ks[20], (DEPTH, FFN_CONV, D_FF), FFN_CONV ** -0.5),
        "ffn_conv_b": nrm(ks[21], (DEPTH, D_FF), 0.02),
        "ffn_w_out": nrm(ks[22], (DEPTH, D_FF, D_MODEL), D_FF ** -0.5),
    }


def reference(x, norm_mix, norm_ffn, attn_w_in, attn_w_out, relpos_table, q_norm_a, k_norm_a,
              q_norm_b, k_norm_b, sinks, ssm_w_in, ssm_conv_w, ssm_conv_b, ssm_dt_bias, ssm_a_log,
              ssm_d, ssm_norm, ssm_w_out, ffn_w_in, ffn_conv_w, ffn_conv_b, ffn_w_out):
    for layer in range(DEPTH):
        i = layer // 2
        h = rmsnorm(x, norm_mix[layer])
        if layer % 2 == 0:
            mix = attn_layer(h, attn_w_in[i], attn_w_out[i], relpos_table[i], q_norm_a[i],
                             k_norm_a[i], q_norm_b[i], k_norm_b[i], sinks[i])
        else:
            mix = ssm_layer(h, ssm_w_in[i], ssm_conv_w[i], ssm_conv_b[i], ssm_dt_bias[i],
                            ssm_a_log[i], ssm_d[i], ssm_norm[i], ssm_w_out[i])
        x = x + mix
        h = rmsnorm(x, norm_ffn[layer])
        x = x + conv_ffn(h, ffn_w_in[layer], ffn_conv_w[layer], ffn_conv_b[layer], ffn_w_out[layer])
    return x
```

```python
import functools

import jax
import jax.numpy as jnp
import numpy as np
from jax import lax
from jax.experimental import pallas as pl
from jax.experimental.pallas import tpu as pltpu

F32 = jnp.float32
BF16 = jnp.bfloat16

LANES = 128
SUBLANES = 8
VMEM_LIMIT_BYTES = 56 * 1024 * 1024

D_MODEL = 1024
EPS = 1e-6
CHUNK = 64
HEAD_DIM = 64
H_A = 8
A_PREV = 8
MAX_REL = 256
H_B = 8
H_B_KV = 2
B_PREV = 2
D_INNER = 2048
SSM_HEADS = 32
SSM_GROUPS = 4
SSM_STATE = 128
SSM_CONV = 4
GROUP_W = D_INNER // SSM_GROUPS
HEADS_PER_GROUP = SSM_HEADS // SSM_GROUPS
D_FF = 2816
FFN_CONV = 3

QBLK = 256
QCH = QBLK // CHUNK
A_BLKS = 3
A_BAND = A_BLKS * QBLK
B_BAND = QBLK + B_PREV * CHUNK

QA0, KA0, QB0, VA0, KB0, VB0 = 0, 512, 1024, 1536, 2048, 2176
QKV_W = 2304
NORM_CHUNKS = ((0, 256), (256, 256), (512, 256), (768, 256), (1024, 256), (1280, 256), (KB0, 128))
COPY_CHUNKS = ((VA0, 512), (VB0, 128))

XBC_W = D_INNER + 2 * SSM_GROUPS * SSM_STATE
SSM_PROJ_W = D_INNER + XBC_W + LANES
SSD_L = 128

TM_PROJ = 512
TM_FFN = 256


def _const_spec(shape):
    zeros = (0,) * len(shape)
    return pl.BlockSpec(shape, lambda *_: zeros, pipeline_mode=pl.Buffered(1))


def _params(sem):
    return pltpu.CompilerParams(dimension_semantics=sem, vmem_limit_bytes=VMEM_LIMIT_BYTES)


def _rms_rows(xf, gain):
    ms = jnp.mean(xf * xf, axis=-1, keepdims=True)
    return xf * lax.rsqrt(ms + EPS) * gain


def _shift_rows(y, prev8, k):
    r = pltpu.roll(y, k, axis=0)
    row = lax.broadcasted_iota(jnp.int32, (SUBLANES, 1), 0)
    first = jnp.where(row < k, pltpu.roll(prev8, k, axis=0), r[:SUBLANES])
    return jnp.concatenate([first, r[SUBLANES:]], axis=0)


def _sigmoid(x):
    return 1.0 / (1.0 + jnp.exp(-x))


def _split3(x):
    hi = x.astype(BF16)
    r1 = x - hi.astype(F32)
    mid = r1.astype(BF16)
    lo = (r1 - mid.astype(F32)).astype(BF16)
    return hi, mid, lo


def _attn_inproj_kernel(x_ref, g_ref, w_ref, gain_ref, bd_ref, o_ref):
    h = _rms_rows(x_ref[...], g_ref[...]).astype(BF16)
    y = jnp.dot(h, w_ref[...], preferred_element_type=F32)
    for c0, w in NORM_CHUNKS:
        yc = y[:, c0:c0 + w]
        sq = yc * yc
        hi = sq.astype(BF16)
        lo = (sq - hi.astype(F32)).astype(BF16)
        bd = bd_ref[:w, :w]
        ss = (jnp.dot(hi, bd, preferred_element_type=F32)
              + jnp.dot(lo, bd, preferred_element_type=F32))
        r = lax.rsqrt(ss * (1.0 / HEAD_DIM) + EPS)
        o_ref[:, c0:c0 + w] = (yc * r * gain_ref[:, c0:c0 + w]).astype(BF16)
    for c0, w in COPY_CHUNKS:
        o_ref[:, c0:c0 + w] = y[:, c0:c0 + w].astype(BF16)


def _attn_inproj(x2, g, w, gain, bd):
    t = x2.shape[0]
    return pl.pallas_call(
        _attn_inproj_kernel,
        grid=(t // TM_PROJ,),
        in_specs=[
            pl.BlockSpec((TM_PROJ, D_MODEL), lambda i: (i, 0)),
            _const_spec((1, D_MODEL)),
            _const_spec((D_MODEL, QKV_W)),
            _const_spec((1, QKV_W)),
            _const_spec((256, 256)),
        ],
        out_specs=pl.BlockSpec((TM_PROJ, QKV_W), lambda i: (i, 0)),
        out_shape=jax.ShapeDtypeStruct((t, QKV_W), BF16),
        compiler_params=_params(("parallel",)),
        name="attn_inproj",
    )(x2, g, w, gain, bd)


def _attention_kernel(sink_ref, qa_ref, ka2_ref, ka1_ref, ka0_ref, va2_ref, va1_ref, va0_ref,
                      qb_ref, kb1_ref, kb0_ref, vb1_ref, vb0_ref, biasa_ref, biasb_ref, o_ref):
    i = pl.program_id(1)
    lane = lax.broadcasted_iota(jnp.int32, (1, LANES), 1)
    lo_half = lane < HEAD_DIM
    half_masks = (lo_half.astype(BF16), (~lo_half).astype(BF16))
    nt = (((1,), (1,)), ((), ()))

    col_a = lax.broadcasted_iota(jnp.int32, (1, A_BAND), 1)
    pad_ok_a = col_a >= (A_BLKS - 1 - i) * QBLK
    for pr in range(H_A // 2):
        sl = slice(pr * LANES, (pr + 1) * LANES)
        k = jnp.concatenate([ka2_ref[:, sl], ka1_ref[:, sl], ka0_ref[:, sl]], axis=0)
        v = jnp.concatenate([va2_ref[:, sl], va1_ref[:, sl], va0_ref[:, sl]], axis=0)
        q = qa_ref[:, sl]
        outs = []
        for e in range(2):
            s = lax.dot_general(q * half_masks[e], k, nt, preferred_element_type=F32)
            s = jnp.where(pad_ok_a, s + biasa_ref[2 * pr + e], -jnp.inf)
            m = jnp.max(s, axis=-1, keepdims=True)
            p = jnp.exp(s - m)
            l = jnp.sum(p, axis=-1, keepdims=True)
            outs.append(jnp.dot(p.astype(BF16), v, preferred_element_type=F32) / l)
        o_ref[:, sl] = jnp.where(lo_half, outs[0], outs[1]).astype(BF16)

    col_b = lax.broadcasted_iota(jnp.int32, (1, B_BAND), 1)
    pad_ok_b = col_b >= (B_BAND - QBLK) - i * QBLK
    kb = jnp.concatenate([kb1_ref[QBLK - B_PREV * CHUNK:, :], kb0_ref[...]], axis=0)
    vb = jnp.concatenate([vb1_ref[QBLK - B_PREV * CHUNK:, :], vb0_ref[...]], axis=0)
    n_slab = H_B // 2
    outs = [[None, None] for _ in range(n_slab)]
    for e in range(2):
        qs = jnp.concatenate(
            [qb_ref[:, j * LANES:(j + 1) * LANES] * half_masks[e] for j in range(n_slab)], axis=0)
        s_all = lax.dot_general(qs, kb, nt, preferred_element_type=F32)
        for j in range(n_slab):
            hd = j + n_slab * e
            snk = sink_ref[hd]
            s = s_all[j * QBLK:(j + 1) * QBLK]
            s = jnp.where(pad_ok_b, s + biasb_ref[hd], -jnp.inf)
            m = jnp.maximum(jnp.max(s, axis=-1, keepdims=True), snk)
            p = jnp.exp(s - m)
            l = jnp.sum(p, axis=-1, keepdims=True) + jnp.exp(snk - m)
            outs[j][e] = jnp.dot(p.astype(BF16), vb, preferred_element_type=F32) / l
    ob0 = H_A * HEAD_DIM
    for j in range(n_slab):
        o_ref[:, ob0 + j * LANES:ob0 + (j + 1) * LANES] = jnp.where(
            lo_half, outs[j][0], outs[j][1]).astype(BF16)


def _attention(qkv, sinks, bias_a, bias_b, batch, seq):
    t = qkv.shape[0]
    nq = seq // QBLK

    def rows(d):
        return lambda b, i: b * nq + jnp.maximum(i - d, 0)

    def spec(width, col, d):
        r = rows(d)
        return pl.BlockSpec((QBLK, width), lambda b, i: (r(b, i), col))

    wa, wb = H_A * HEAD_DIM, H_B_KV * HEAD_DIM
    in_specs = [
        pl.BlockSpec(memory_space=pltpu.SMEM),
        spec(wa, QA0 // wa, 0),
        spec(wa, KA0 // wa, 2), spec(wa, KA0 // wa, 1), spec(wa, KA0 // wa, 0),
        spec(wa, VA0 // wa, 2), spec(wa, VA0 // wa, 1), spec(wa, VA0 // wa, 0),
        spec(wa, QB0 // wa, 0),
        spec(wb, KB0 // wb, 1), spec(wb, KB0 // wb, 0),
        spec(wb, VB0 // wb, 1), spec(wb, VB0 // wb, 0),
        _const_spec((H_A, QBLK, A_BAND)),
        _const_spec((H_B, QBLK, B_BAND)),
    ]
    return pl.pallas_call(
        _attention_kernel,
        grid=(batch, nq),
        in_specs=in_specs,
        out_specs=pl.BlockSpec((QBLK, D_MODEL), lambda b, i: (b * nq + i, 0)),
        out_shape=jax.ShapeDtypeStruct((t, D_MODEL), BF16),
        compiler_params=_params(("parallel", "parallel")),
        name="attention",
    )(sinks, qkv, qkv, qkv, qkv, qkv, qkv, qkv, qkv, qkv, qkv, qkv, qkv, bias_a, bias_b)


def _mix_ffn_kernel(mix_ref, x_ref, wo_ref, gn_ref, wg_ref, wu_ref, cw_ref, cb_ref, wd_ref,
                    o_ref, carry_ref):
    @pl.when(pl.program_id(1) == 0)
    def _():
        carry_ref[...] = jnp.zeros_like(carry_ref)

    x1 = x_ref[...] + jnp.dot(mix_ref[...], wo_ref[...], preferred_element_type=F32)
    h = _rms_rows(x1, gn_ref[...]).astype(BF16)
    g = jnp.dot(h, wg_ref[...], preferred_element_type=F32)
    u = jnp.dot(h, wu_ref[...], preferred_element_type=F32)
    prev8 = carry_ref[...]
    gc = (cw_ref[0:1, :] * _shift_rows(g, prev8, 2) + cw_ref[1:2, :] * _shift_rows(g, prev8, 1)
          + cw_ref[2:3, :] * g + cb_ref[...])
    carry_ref[...] = g[g.shape[0] - SUBLANES:]
    act = (gc * _sigmoid(gc) * u).astype(BF16)
    o_ref[...] = x1 + jnp.dot(act, wd_ref[...], preferred_element_type=F32)


def _mix_ffn(mix, x2, wo, gn, wg, wu, cw, cb, wd, batch, seq):
    t, kmix = mix.shape
    nt = seq // TM_FFN
    row = lambda b, j: (b * nt + j, 0)
    return pl.pallas_call(
        _mix_ffn_kernel,
        grid=(batch, nt),
        in_specs=[
            pl.BlockSpec((TM_FFN, kmix), row),
            pl.BlockSpec((TM_FFN, D_MODEL), row),
            _const_spec((kmix, D_MODEL)),
            _const_spec((1, D_MODEL)),
            _const_spec((D_MODEL, D_FF)),
            _const_spec((D_MODEL, D_FF)),
            _const_spec((FFN_CONV, D_FF)),
            _const_spec((1, D_FF)),
            _const_spec((D_FF, D_MODEL)),
        ],
        out_specs=pl.BlockSpec((TM_FFN, D_MODEL), row),
        out_shape=jax.ShapeDtypeStruct((t, D_MODEL), F32),
        scratch_shapes=[pltpu.VMEM((SUBLANES, D_FF), F32)],
        compiler_params=_params(("parallel", "arbitrary")),
        name="mix_ffn",
    )(mix, x2, wo, gn, wg, wu, cw, cb, wd)


SSM_NCH = 512


def _ssm_inproj_kernel(x_ref, g_ref, w_ref, cw_ref, cb_ref, dtb_ref, z_ref, xbc_ref, dt_ref,
                       carry_ref):
    @pl.when(pl.program_id(1) == 0)
    def _():
        carry_ref[...] = jnp.zeros_like(carry_ref)

    h = _rms_rows(x_ref[...], g_ref[...]).astype(BF16)
    for c in range(D_INNER // SSM_NCH):
        cs = slice(c * SSM_NCH, (c + 1) * SSM_NCH)
        z_ref[:, cs] = jnp.dot(h, w_ref[:, cs], preferred_element_type=F32).astype(BF16)
    for c in range(XBC_W // SSM_NCH):
        cs = slice(c * SSM_NCH, (c + 1) * SSM_NCH)
        y = jnp.dot(h, w_ref[:, D_INNER + c * SSM_NCH:D_INNER + (c + 1) * SSM_NCH],
                    preferred_element_type=F32)
        prev8 = carry_ref[:, cs]
        acc = cw_ref[SSM_CONV - 1:SSM_CONV, cs] * y + cb_ref[:, cs]
        for k in range(1, SSM_CONV):
            acc = acc + cw_ref[SSM_CONV - 1 - k:SSM_CONV - k, cs] * _shift_rows(y, prev8, k)
        carry_ref[:, cs] = y[y.shape[0] - SUBLANES:]
        xbc_ref[:, cs] = (acc * _sigmoid(acc)).astype(BF16)
    raw = jnp.dot(h, w_ref[:, D_INNER + XBC_W:], preferred_element_type=F32) + dtb_ref[...]
    dt_ref[...] = jnp.maximum(raw, 0.0) + jnp.log1p(jnp.exp(-jnp.abs(raw)))


def _ssm_inproj(x2, g, w, cw, cb, dtb, batch, seq):
    t = x2.shape[0]
    nt = seq // TM_PROJ
    row = lambda b, j: (b * nt + j, 0)
    return pl.pallas_call(
        _ssm_inproj_kernel,
        grid=(batch, nt),
        in_specs=[
            pl.BlockSpec((TM_PROJ, D_MODEL), row),
            _const_spec((1, D_MODEL)),
            _const_spec((D_MODEL, SSM_PROJ_W)),
            _const_spec((SSM_CONV, XBC_W)),
            _const_spec((1, XBC_W)),
            _const_spec((1, LANES)),
        ],
        out_specs=[
            pl.BlockSpec((TM_PROJ, D_INNER), row),
            pl.BlockSpec((TM_PROJ, XBC_W), row),
            pl.BlockSpec((TM_PROJ, LANES), row),
        ],
        out_shape=[
            jax.ShapeDtypeStruct((t, D_INNER), BF16),
            jax.ShapeDtypeStruct((t, XBC_W), BF16),
            jax.ShapeDtypeStruct((t, LANES), F32),
        ],
        scratch_shapes=[pltpu.VMEM((SUBLANES, XBC_W), F32)],
        compiler_params=_params(("parallel", "arbitrary")),
        name="ssm_inproj",
    )(x2, g, w, cw, cb, dtb)


def _ssd_kernel(x_ref, b_ref, c_ref, z_ref, dt_ref, alog_ref, dskip_ref, nw_ref, tri_ref, exp_ref,
                o_ref, state_ref):
    g = pl.program_id(1)

    @pl.when(pl.program_id(2) == 0)
    def _():
        state_ref[...] = jnp.zeros_like(state_ref)

    ll = SSD_L
    shift = (LANES - HEADS_PER_GROUP * g) % LANES
    dt = pltpu.roll(dt_ref[...], shift, axis=1)
    a = -jnp.exp(pltpu.roll(alog_ref[...], shift, axis=1))
    da = dt * a
    tri = tri_ref[...]
    acs = jnp.dot(tri, jnp.concatenate(_split3(da), axis=1), preferred_element_type=F32)
    acs = acs[:, :LANES] + acs[:, LANES:2 * LANES] + acs[:, 2 * LANES:]
    da_t = jnp.transpose(da)
    tri_nt = (((1,), (1,)), ((), ()))
    acs_t = lax.dot_general(jnp.concatenate(_split3(da_t), axis=0), tri, tri_nt,
                            preferred_element_type=F32)
    acs_t = acs_t[:LANES] + acs_t[LANES:2 * LANES] + acs_t[2 * LANES:]
    dt_t = jnp.transpose(dt)
    last = acs[ll - 1:ll, :]
    small = jnp.concatenate([jnp.exp(acs), jnp.exp(last - acs) * dt], axis=0)
    big = jnp.dot(jnp.concatenate(_split3(small), axis=1), exp_ref[...],
                  preferred_element_type=F32)
    ea_e, w_e = big[:ll], big[ll:]

    xb = x_ref[...]
    xf = xb.astype(F32)
    bmat = b_ref[...]
    cmat = c_ref[...]
    cb = lax.dot_general(cmat, bmat, tri_nt, preferred_element_type=F32)
    li = lax.broadcasted_iota(jnp.int32, (ll, ll), 0)
    si = lax.broadcasted_iota(jnp.int32, (ll, ll), 1)
    causal = li >= si
    lane = lax.broadcasted_iota(jnp.int32, (1, LANES), 1)
    lo_half = lane < HEAD_DIM

    state = state_ref[...]
    y_state = jnp.dot(cmat, state.astype(BF16), preferred_element_type=F32) * ea_e
    y_intra = []
    for pr in range(HEADS_PER_GROUP // 2):
        xs = xb[:, pr * LANES:(pr + 1) * LANES]
        ys = []
        for e in range(2):
            hd = 2 * pr + e
            seg = acs[:, hd:hd + 1] - acs_t[hd:hd + 1, :]
            m = cb * jnp.exp(jnp.where(causal, seg, -jnp.inf)) * dt_t[hd:hd + 1, :]
            ys.append(jnp.dot(m.astype(BF16), xs, preferred_element_type=F32))
        y_intra.append(jnp.where(lo_half, ys[0], ys[1]))
    y = jnp.concatenate(y_intra, axis=1) + y_state + dskip_ref[...] * xf
    zf = z_ref[...].astype(F32)
    o_ref[...] = _rms_rows(y * (zf * _sigmoid(zf)), nw_ref[...]).astype(BF16)

    new_state = state * ea_e[ll - 1:ll, :] + lax.dot_general(
        bmat, (xf * w_e).astype(BF16), (((0,), (0,)), ((), ())), preferred_element_type=F32)
    state_ref[...] = new_state


def _ssd(z, xbc, dt, alog, dskip, nw, tri, expand, batch, seq):
    t = z.shape[0]
    nc = seq // SSD_L
    row = lambda b, g, c: b * nc + c
    return pl.pallas_call(
        _ssd_kernel,
        grid=(batch, SSM_GROUPS, nc),
        in_specs=[
            pl.BlockSpec((SSD_L, GROUP_W), lambda b, g, c: (row(b, g, c), g)),
            pl.BlockSpec((SSD_L, SSM_STATE), lambda b, g, c: (row(b, g, c), D_INNER // SSM_STATE + g)),
            pl.BlockSpec((SSD_L, SSM_STATE),
                         lambda b, g, c: (row(b, g, c), D_INNER // SSM_STATE + SSM_GROUPS + g)),
            pl.BlockSpec((SSD_L, GROUP_W), lambda b, g, c: (row(b, g, c), g)),
            pl.BlockSpec((SSD_L, LANES), lambda b, g, c: (row(b, g, c), 0)),
            _const_spec((1, LANES)),
            pl.BlockSpec((1, GROUP_W), lambda b, g, c: (0, g)),
            pl.BlockSpec((1, GROUP_W), lambda b, g, c: (0, g)),
            _const_spec((SSD_L, SSD_L)),
            _const_spec((3 * LANES, GROUP_W)),
        ],
        out_specs=pl.BlockSpec((SSD_L, GROUP_W), lambda b, g, c: (row(b, g, c), g)),
        out_shape=jax.ShapeDtypeStruct((t, D_INNER), BF16),
        scratch_shapes=[pltpu.VMEM((SSM_STATE, GROUP_W), F32)],
        compiler_params=_params(("parallel", "parallel", "arbitrary")),
        name="ssd",
    )(xbc, xbc, xbc, z, dt, alog, dskip, nw, tri, expand)


def _attn_weights(w_in, w_out, q_norm_a, k_norm_a, q_norm_b, k_norm_b):
    da = H_A * HEAD_DIM
    qa, ka, va, qb, kb, vb = jnp.split(w_in, [da, 2 * da, 3 * da, 4 * da, 4 * da + 128], axis=1)
    perm = np.concatenate([np.arange(HEAD_DIM) + (j + 4 * e) * HEAD_DIM
                           for j in range(H_B // 2) for e in range(2)])
    w = jnp.concatenate([qa, ka, qb[:, perm], va, kb, vb], axis=1).astype(BF16)
    scale = HEAD_DIM ** -0.5
    ones = jnp.ones((1,), F32)
    gain = jnp.concatenate([
        jnp.tile(q_norm_a, H_A) * scale, jnp.tile(k_norm_a, H_A), jnp.tile(q_norm_b, H_B) * scale,
        jnp.tile(ones, da), jnp.tile(k_norm_b, H_B_KV), jnp.tile(ones, 128)]).reshape(1, QKV_W)
    wo = jnp.concatenate([w_out[:da], w_out[da:][perm]], axis=0).astype(BF16)
    return w, gain.astype(F32), wo


def _attn_bias(relpos_table):
    neg = -jnp.inf
    r = np.arange(QBLK)[:, None]
    jj = np.arange(A_BAND)[None, :]
    rel = r - jj + (A_BLKS - 1) * QBLK
    dchunk = jj // CHUNK - r // CHUNK
    ok = (dchunk >= 0) & (dchunk <= A_PREV)
    idx = np.clip(rel, -MAX_REL, MAX_REL) + MAX_REL
    bias_a = jnp.where(ok[None], relpos_table.astype(F32)[:, idx], neg)
    jb = np.arange(B_BAND)[None, :]
    relb = r - jb + B_PREV * CHUNK
    dcb = jb // CHUNK - r // CHUNK
    okb = (dcb >= 0) & (dcb <= B_PREV)
    slopes = 2.0 ** (-8.0 * jnp.arange(1, H_B + 1, dtype=F32) / H_B)
    bias_b = jnp.where(okb[None], -slopes[:, None, None] * jnp.abs(relb).astype(F32)[None], neg)
    return bias_a, bias_b


def _block_diag_ones():
    i = np.arange(256)
    return jnp.asarray((i[:, None] // HEAD_DIM) == (i[None, :] // HEAD_DIM), dtype=BF16)


def _ssd_constants():
    i = np.arange(SSD_L)
    tri = jnp.asarray(i[None, :] <= i[:, None], dtype=BF16)
    e = np.zeros((LANES, GROUP_W), np.float32)
    for hd in range(HEADS_PER_GROUP):
        e[hd, hd * HEAD_DIM:(hd + 1) * HEAD_DIM] = 1.0
    expand = jnp.asarray(np.concatenate([e, e, e], axis=0), dtype=BF16)
    return tri, expand


def kernel(x, norm_mix, norm_ffn, attn_w_in, attn_w_out, relpos_table, q_norm_a, k_norm_a, q_norm_b,
           k_norm_b, sinks, ssm_w_in, ssm_conv_w, ssm_conv_b, ssm_dt_bias, ssm_a_log, ssm_d, ssm_norm,
           ssm_w_out, ffn_w_in, ffn_conv_w, ffn_conv_b, ffn_w_out):
    batch, seq, _ = x.shape
    assert seq % TM_PROJ == 0 and seq % QBLK == 0 and seq % SSD_L == 0
    x2 = x.reshape(batch * seq, D_MODEL)
    row = lambda v: v.reshape(1, -1).astype(F32)

    def ffn(layer, mix, xin, wo):
        wg, wu = ffn_w_in[layer, :, :D_FF], ffn_w_in[layer, :, D_FF:]
        return _mix_ffn(mix, xin, wo, row(norm_ffn[layer]), wg.astype(BF16), wu.astype(BF16),
                        ffn_conv_w[layer].astype(F32), row(ffn_conv_b[layer]),
                        ffn_w_out[layer].astype(BF16), batch, seq)

    w, gain, wo = _attn_weights(attn_w_in[0], attn_w_out[0], q_norm_a[0], k_norm_a[0], q_norm_b[0],
                                k_norm_b[0])
    qkv = _attn_inproj(x2, row(norm_mix[0]), w, gain, _block_diag_ones())
    bias_a, bias_b = _attn_bias(relpos_table[0])
    heads = _attention(qkv, sinks[0].astype(F32), bias_a, bias_b, batch, seq)
    x2 = ffn(0, heads, x2, wo)

    pad = SSM_PROJ_W - ssm_w_in.shape[2]
    w_ssm = jnp.pad(ssm_w_in[0], ((0, 0), (0, pad))).astype(BF16)
    pad_h = LANES - SSM_HEADS
    dtb = jnp.pad(ssm_dt_bias[0], (0, pad_h)).reshape(1, LANES).astype(F32)
    alog = jnp.pad(ssm_a_log[0], (0, pad_h)).reshape(1, LANES).astype(F32)
    z, xbc, dt = _ssm_inproj(x2, row(norm_mix[1]), w_ssm, ssm_conv_w[0].astype(F32),
                             row(ssm_conv_b[0]), dtb, batch, seq)
    tri, expand = _ssd_constants()
    dskip = row(jnp.repeat(ssm_d[0], HEAD_DIM))
    y = _ssd(z, xbc, dt, alog, dskip, row(ssm_norm[0]), tri, expand, batch, seq)
    x2 = ffn(1, y, x2, ssm_w_out[0].astype(BF16))
    return x2.reshape(batch, seq, D_MODEL)
```

```python
import functools

import jax
import jax.numpy as jnp
import numpy as np
from jax import lax
from jax.experimental import pallas as pl
from jax.experimental.pallas import tpu as pltpu

F32 = jnp.float32
BF16 = jnp.bfloat16

LANES = 128
SUBLANES = 8
VMEM_LIMIT_BYTES = 56 * 1024 * 1024

D_MODEL = 1024
EPS = 1e-6
CHUNK = 64
HEAD_DIM = 64
H_A = 8
A_PREV = 8
MAX_REL = 256
H_B = 8
H_B_KV = 2
B_PREV = 2
D_INNER = 2048
SSM_HEADS = 32
SSM_GROUPS = 4
SSM_STATE = 128
SSM_CONV = 4
GROUP_W = D_INNER // SSM_GROUPS
HEADS_PER_GROUP = SSM_HEADS // SSM_GROUPS
D_FF = 2816
FFN_CONV = 3

QBLK = 256
QCH = QBLK // CHUNK
A_BLKS = 3
A_BAND = A_BLKS * QBLK
B_BAND = QBLK + B_PREV * CHUNK

QA0, KA0, QB0, VA0, KB0, VB0 = 0, 512, 1024, 1536, 2048, 2176
QKV_W = 2304
NORM_CHUNKS = ((0, 256), (256, 256), (512, 256), (768, 256), (1024, 256), (1280, 256), (KB0, 128))
COPY_CHUNKS = ((VA0, 512), (VB0, 128))

XBC_W = D_INNER + 2 * SSM_GROUPS * SSM_STATE
SSM_PROJ_W = D_INNER + XBC_W + LANES
SSD_L = 128

TM_PROJ = 512
TM_FFN = 256


def _const_spec(shape):
    zeros = (0,) * len(shape)
    return pl.BlockSpec(shape, lambda *_: zeros, pipeline_mode=pl.Buffered(1))


def _params(sem):
    return pltpu.CompilerParams(dimension_semantics=sem, vmem_limit_bytes=VMEM_LIMIT_BYTES)


def _rms_rows(xf, gain):
    ms = jnp.mean(xf * xf, axis=-1, keepdims=True)
    return xf * lax.rsqrt(ms + EPS) * gain


def _shift_rows(y, prev8, k):
    r = pltpu.roll(y, k, axis=0)
    row = lax.broadcasted_iota(jnp.int32, (SUBLANES, 1), 0)
    first = jnp.where(row < k, pltpu.roll(prev8, k, axis=0), r[:SUBLANES])
    return jnp.concatenate([first, r[SUBLANES:]], axis=0)


def _sigmoid(x):
    return 1.0 / (1.0 + jnp.exp(-x))


def _split3(x):
    hi = x.astype(BF16)
    r1 = x - hi.astype(F32)
    mid = r1.astype(BF16)
    lo = (r1 - mid.astype(F32)).astype(BF16)
    return hi, mid, lo


def _attn_inproj_kernel(x_ref, g_ref, w_ref, gain_ref, bd_ref, o_ref):
    h = _rms_rows(x_ref[...], g_ref[...]).astype(BF16)
    y = jnp.dot(h, w_ref[...], preferred_element_type=F32)
    for c0, w in NORM_CHUNKS:
        yc = y[:, c0:c0 + w]
        sq = yc * yc
        hi = sq.astype(BF16)
        lo = (sq - hi.astype(F32)).astype(BF16)
        bd = bd_ref[:w, :w]
        ss = (jnp.dot(hi, bd, preferred_element_type=F32)
              + jnp.dot(lo, bd, preferred_element_type=F32))
        r = lax.rsqrt(ss * (1.0 / HEAD_DIM) + EPS)
        o_ref[:, c0:c0 + w] = (yc * r * gain_ref[:, c0:c0 + w]).astype(BF16)
    for c0, w in COPY_CHUNKS:
        o_ref[:, c0:c0 + w] = y[:, c0:c0 + w].astype(BF16)


def _attn_inproj(x2, g, w, gain, bd):
    t = x2.shape[0]
    return pl.pallas_call(
        _attn_inproj_kernel,
        grid=(t // TM_PROJ,),
        in_specs=[
            pl.BlockSpec((TM_PROJ, D_MODEL), lambda i: (i, 0)),
            _const_spec((1, D_MODEL)),
            _const_spec((D_MODEL, QKV_W)),
            _const_spec((1, QKV_W)),
            _const_spec((256, 256)),
        ],
        out_specs=pl.BlockSpec((TM_PROJ, QKV_W), lambda i: (i, 0)),
        out_shape=jax.ShapeDtypeStruct((t, QKV_W), BF16),
        compiler_params=_params(("parallel",)),
        name="attn_inproj",
    )(x2, g, w, gain, bd)


def _attention_kernel(sink_ref, qa_ref, ka2_ref, ka1_ref, ka0_ref, va2_ref, va1_ref, va0_ref,
                      qb_ref, kb1_ref, kb0_ref, vb1_ref, vb0_ref, biasa_ref, biasb_ref, o_ref):
    i = pl.program_id(1)
    lane = lax.broadcasted_iota(jnp.int32, (1, LANES), 1)
    lo_half = lane < HEAD_DIM
    half_masks = (lo_half.astype(BF16), (~lo_half).astype(BF16))
    nt = (((1,), (1,)), ((), ()))

    col_a = lax.broadcasted_iota(jnp.int32, (1, A_BAND), 1)
    pad_ok_a = col_a >= (A_BLKS - 1 - i) * QBLK
    for pr in range(H_A // 2):
        sl = slice(pr * LANES, (pr + 1) * LANES)
        k = jnp.concatenate([ka2_ref[:, sl], ka1_ref[:, sl], ka0_ref[:, sl]], axis=0)
        v = jnp.concatenate([va2_ref[:, sl], va1_ref[:, sl], va0_ref[:, sl]], axis=0)
        q = qa_ref[:, sl]
        outs = []
        for e in range(2):
            s = lax.dot_general(q * half_masks[e], k, nt, preferred_element_type=F32)
            s = jnp.where(pad_ok_a, s + biasa_ref[2 * pr + e], -jnp.inf)
            m = jnp.max(s, axis=-1, keepdims=True)
            p = jnp.exp(s - m)
            l = jnp.sum(p, axis=-1, keepdims=True)
            outs.append(jnp.dot(p.astype(BF16), v, preferred_element_type=F32) / l)
        o_ref[:, sl] = jnp.where(lo_half, outs[0], outs[1]).astype(BF16)

    col_b = lax.broadcasted_iota(jnp.int32, (1, B_BAND), 1)
    pad_ok_b = col_b >= (B_BAND - QBLK) - i * QBLK
    kb = jnp.concatenate([kb1_ref[QBLK - B_PREV * CHUNK:, :], kb0_ref[...]], axis=0)
    vb = jnp.concatenate([vb1_ref[QBLK - B_PREV * CHUNK:, :], vb0_ref[...]], axis=0)
    n_slab = H_B // 2
    outs = [[None, None] for _ in range(n_slab)]
    for e in range(2):
        qs = jnp.concatenate(
            [qb_ref[:, j * LANES:(j + 1) * LANES] * half_masks[e] for j in range(n_slab)], axis=0)
        s_all = lax.dot_general(qs, kb, nt, preferred_element_type=F32)
        for j in range(n_slab):
            hd = j + n_slab * e
            snk = sink_ref[hd]
            s = s_all[j * QBLK:(j + 1) * QBLK]
            s = jnp.where(pad_ok_b, s + biasb_ref[hd], -jnp.inf)
            m = jnp.maximum(jnp.max(s, axis=-1, keepdims=True), snk)
            p = jnp.exp(s - m)
            l = jnp.sum(p, axis=-1, keepdims=True) + jnp.exp(snk - m)
            outs[j][e] = jnp.dot(p.astype(BF16), vb, preferred_element_type=F32) / l
    ob0 = H_A * HEAD_DIM
    for j in range(n_slab):
        o_ref[:, ob0 + j * LANES:ob0 + (j + 1) * LANES] = jnp.where(
            lo_half, outs[j][0], outs[j][1]).astype(BF16)


def _attention(qkv, sinks, bias_a, bias_b, batch, seq):
    t = qkv.shape[0]
    nq = seq // QBLK

    def rows(d):
        return lambda b, i: b * nq + jnp.maximum(i - d, 0)

    def spec(width, col, d):
        r = rows(d)
        return pl.BlockSpec((QBLK, width), lambda b, i: (r(b, i), col))

    wa, wb = H_A * HEAD_DIM, H_B_KV * HEAD_DIM
    in_specs = [
        pl.BlockSpec(memory_space=pltpu.SMEM),
        spec(wa, QA0 // wa, 0),
        spec(wa, KA0 // wa, 2), spec(wa, KA0 // wa, 1), spec(wa, KA0 // wa, 0),
        spec(wa, VA0 // wa, 2), spec(wa, VA0 // wa, 1), spec(wa, VA0 // wa, 0),
        spec(wa, QB0 // wa, 0),
        spec(wb, KB0 // wb, 1), spec(wb, KB0 // wb, 0),
        spec(wb, VB0 // wb, 1), spec(wb, VB0 // wb, 0),
        _const_spec((H_A, QBLK, A_BAND)),
        _const_spec((H_B, QBLK, B_BAND)),
    ]
    return pl.pallas_call(
        _attention_kernel,
        grid=(batch, nq),
        in_specs=in_specs,
        out_specs=pl.BlockSpec((QBLK, D_MODEL), lambda b, i: (b * nq + i, 0)),
        out_shape=jax.ShapeDtypeStruct((t, D_MODEL), BF16),
        compiler_params=_params(("parallel", "parallel")),
        name="attention",
    )(sinks, qkv, qkv, qkv, qkv, qkv, qkv, qkv, qkv, qkv, qkv, qkv, qkv, bias_a, bias_b)


def _mix_ffn_kernel(mix_ref, x_ref, wo_ref, gn_ref, wg_ref, wu_ref, cw_ref, cb_ref, wd_ref,
                    o_ref, carry_ref):
    @pl.when(pl.program_id(1) == 0)
    def _():
        carry_ref[...] = jnp.zeros_like(carry_ref)

    x1 = x_ref[...] + jnp.dot(mix_ref[...], wo_ref[...], preferred_element_type=F32)
    h = _rms_rows(x1, gn_ref[...]).astype(BF16)
    g = jnp.dot(h, wg_ref[...], preferred_element_type=F32)
    u = jnp.dot(h, wu_ref[...], preferred_element_type=F32)
    prev8 = carry_ref[...]
    gc = (cw_ref[0:1, :] * _shift_rows(g, prev8, 2) + cw_ref[1:2, :] * _shift_rows(g, prev8, 1)
          + cw_ref[2:3, :] * g + cb_ref[...])
    carry_ref[...] = g[g.shape[0] - SUBLANES:]
    act = (gc * _sigmoid(gc) * u).astype(BF16)
    o_ref[...] = x1 + jnp.dot(act, wd_ref[...], preferred_element_type=F32)


def _mix_ffn(mix, x2, wo, gn, wg, wu, cw, cb, wd, batch, seq):
    t, kmix = mix.shape
    nt = seq // TM_FFN
    row = lambda b, j: (b * nt + j, 0)
    return pl.pallas_call(
        _mix_ffn_kernel,
        grid=(batch, nt),
        in_specs=[
            pl.BlockSpec((TM_FFN, kmix), row),
            pl.BlockSpec((TM_FFN, D_MODEL), row),
            _const_spec((kmix, D_MODEL)),
            _const_spec((1, D_MODEL)),
            _const_spec((D_MODEL, D_FF)),
            _const_spec((D_MODEL, D_FF)),
            _const_spec((FFN_CONV, D_FF)),
            _const_spec((1, D_FF)),
            _const_spec((D_FF, D_MODEL)),
        ],
        out_specs=pl.BlockSpec((TM_FFN, D_MODEL), row),
        out_shape=jax.ShapeDtypeStruct((t, D_MODEL), F32),
        scratch_shapes=[pltpu.VMEM((SUBLANES, D_FF), F32)],
        compiler_params=_params(("parallel", "arbitrary")),
        name="mix_ffn",
    )(mix, x2, wo, gn, wg, wu, cw, cb, wd)


SSM_NCH = 512


def _ssm_inproj_kernel(x_ref, g_ref, w_ref, cw_ref, cb_ref, dtb_ref, z_ref, xbc_ref, dt_ref,
                       carry_ref):
    @pl.when(pl.program_id(1) == 0)
    def _():
        carry_ref[...] = jnp.zeros_like(carry_ref)

    h = _rms_rows(x_ref[...], g_ref[...]).astype(BF16)
    for c in range(D_INNER // SSM_NCH):
        cs = slice(c * SSM_NCH, (c + 1) * SSM_NCH)
        z_ref[:, cs] = jnp.dot(h, w_ref[:, cs], preferred_element_type=F32).astype(BF16)
    for c in range(XBC_W // SSM_NCH):
        cs = slice(c * SSM_NCH, (c + 1) * SSM_NCH)
        y = jnp.dot(h, w_ref[:, D_INNER + c * SSM_NCH:D_INNER + (c + 1) * SSM_NCH],
                    preferred_element_type=F32)
        prev8 = carry_ref[:, cs]
        acc = cw_ref[SSM_CONV - 1:SSM_CONV, cs] * y + cb_ref[:, cs]
        for k in range(1, SSM_CONV):
            acc = acc + cw_ref[SSM_CONV - 1 - k:SSM_CONV - k, cs] * _shift_rows(y, prev8, k)
        carry_ref[:, cs] = y[y.shape[0] - SUBLANES:]
        xbc_ref[:, cs] = (acc * _sigmoid(acc)).astype(BF16)
    raw = jnp.dot(h, w_ref[:, D_INNER + XBC_W:], preferred_element_type=F32) + dtb_ref[...]
    dt_ref[...] = jnp.maximum(raw, 0.0) + jnp.log1p(jnp.exp(-jnp.abs(raw)))


def _ssm_inproj(x2, g, w, cw, cb, dtb, batch, seq):
    t = x2.shape[0]
    nt = seq // TM_PROJ
    row = lambda b, j: (b * nt + j, 0)
    return pl.pallas_call(
        _ssm_inproj_kernel,
        grid=(batch, nt),
        in_specs=[
            pl.BlockSpec((TM_PROJ, D_MODEL), row),
            _const_spec((1, D_MODEL)),
            _const_spec((D_MODEL, SSM_PROJ_W)),
            _const_spec((SSM_CONV, XBC_W)),
            _const_spec((1, XBC_W)),
            _const_spec((1, LANES)),
        ],
        out_specs=[
            pl.BlockSpec((TM_PROJ, D_INNER), row),
            pl.BlockSpec((TM_PROJ, XBC_W), row),
            pl.BlockSpec((TM_PROJ, LANES), row),
        ],
        out_shape=[
            jax.ShapeDtypeStruct((t, D_INNER), BF16),
            jax.ShapeDtypeStruct((t, XBC_W), BF16),
            jax.ShapeDtypeStruct((t, LANES), F32),
        ],
        scratch_shapes=[pltpu.VMEM((SUBLANES, XBC_W), F32)],
        compiler_params=_params(("parallel", "arbitrary")),
        name="ssm_inproj",
    )(x2, g, w, cw, cb, dtb)


def _cumsum_rows(a):
    n = a.shape[0]
    row = lax.broadcasted_iota(jnp.int32, a.shape, 0)
    d = 1
    while d < n:
        if d < SUBLANES:
            shifted = jnp.where(row < d, 0.0, pltpu.roll(a, d, axis=0))
        else:
            shifted = jnp.concatenate([jnp.zeros((d, a.shape[1]), F32), a[:n - d]], axis=0)
        a = a + shifted
        d *= 2
    return a


HEADS_PER_DOT = 4


def _ssd_kernel(xbc_ref, z_ref, dt_ref, alog_ref, dskip_ref, nw_ref, exp_ref, o_ref, state_ref):
    @pl.when(pl.program_id(1) == 0)
    def _():
        state_ref[...] = jnp.zeros_like(state_ref)

    ll = SSD_L
    nt = (((1,), (1,)), ((), ()))
    dt = dt_ref[...]
    acs = _cumsum_rows(dt * -jnp.exp(alog_ref[...]))
    acs_t = jnp.transpose(acs)
    dt_t = jnp.transpose(dt)
    last = acs[ll - 1:ll, :]
    ea = jnp.exp(acs)

    lane = lax.broadcasted_iota(jnp.int32, (1, LANES), 1)
    small = jnp.concatenate([jnp.exp(last - acs) * dt, jnp.broadcast_to(ea[ll - 1:ll, :], (SUBLANES, LANES))],
                            axis=0)
    small = jnp.where(lane < SSM_HEADS, small, 0.0)
    hi = small.astype(BF16).astype(F32)
    r1 = small - hi
    mid = r1.astype(BF16).astype(F32)
    lo = (r1 - mid).astype(BF16).astype(F32)
    packed = (hi + pltpu.roll(mid, SSM_HEADS, axis=1) + pltpu.roll(lo, 2 * SSM_HEADS, axis=1)).astype(BF16)
    big = jnp.dot(packed, exp_ref[...], preferred_element_type=F32)
    w_e = big[:ll]
    decay_e = big[ll:ll + 1]

    causal = (lax.broadcasted_iota(jnp.int32, (ll, ll), 0) >= lax.broadcasted_iota(jnp.int32, (ll, ll), 1))
    lane_w = lax.broadcasted_iota(jnp.int32, (1, HEADS_PER_DOT * HEAD_DIM), 1)
    state = state_ref[...]
    state_b = state.astype(BF16)
    for g in range(SSM_GROUPS):
        c0 = g * GROUP_W
        bmat = xbc_ref[:, D_INNER + g * SSM_STATE:D_INNER + (g + 1) * SSM_STATE]
        c1 = D_INNER + (SSM_GROUPS + g) * SSM_STATE
        cmat = xbc_ref[:, c1:c1 + SSM_STATE]
        cb = jnp.where(causal, lax.dot_general(cmat, bmat, nt, preferred_element_type=F32), 0.0)
        cf = cmat.astype(F32)
        ys = []
        for part in range(HEADS_PER_GROUP // HEADS_PER_DOT):
            cc = c0 + part * HEADS_PER_DOT * HEAD_DIM
            wd = HEADS_PER_DOT * HEAD_DIM
            lhs = []
            for hh in range(HEADS_PER_DOT):
                hd = g * HEADS_PER_GROUP + part * HEADS_PER_DOT + hh
                seg = jnp.minimum(acs[:, hd:hd + 1] - acs_t[hd:hd + 1, :], 0.0)
                m = cb * jnp.exp(seg) * dt_t[hd:hd + 1, :]
                cea = cf * ea[:, hd:hd + 1]
                lhs.append(jnp.concatenate([m.astype(BF16), cea.astype(BF16)], axis=1))
            rhs = jnp.concatenate([xbc_ref[:, cc:cc + wd], state_b[:, cc:cc + wd]], axis=0)
            out = jnp.dot(jnp.concatenate(lhs, axis=0), rhs, preferred_element_type=F32)
            y = out[(HEADS_PER_DOT - 1) * ll:]
            for hh in range(HEADS_PER_DOT - 2, -1, -1):
                y = jnp.where(lane_w < (hh + 1) * HEAD_DIM, out[hh * ll:(hh + 1) * ll], y)
            ys.append(y)
        xf = xbc_ref[:, c0:c0 + GROUP_W].astype(F32)
        y = jnp.concatenate(ys, axis=1) + dskip_ref[:, c0:c0 + GROUP_W] * xf
        zf = z_ref[:, c0:c0 + GROUP_W].astype(F32)
        o_ref[:, c0:c0 + GROUP_W] = _rms_rows(y * (zf * _sigmoid(zf)),
                                              nw_ref[:, c0:c0 + GROUP_W]).astype(BF16)
        xw = (xf * w_e[:, c0:c0 + GROUP_W]).astype(BF16)
        state_ref[:, c0:c0 + GROUP_W] = (
            state[:, c0:c0 + GROUP_W] * decay_e[:, c0:c0 + GROUP_W]
            + lax.dot_general(bmat, xw, (((0,), (0,)), ((), ())), preferred_element_type=F32))


def _ssd(z, xbc, dt, alog, dskip, nw, expand, batch, seq):
    t = z.shape[0]
    nc = seq // SSD_L
    row = lambda b, c: (b * nc + c, 0)
    return pl.pallas_call(
        _ssd_kernel,
        grid=(batch, nc),
        in_specs=[
            pl.BlockSpec((SSD_L, XBC_W), row),
            pl.BlockSpec((SSD_L, D_INNER), row),
            pl.BlockSpec((SSD_L, LANES), row),
            _const_spec((1, LANES)),
            _const_spec((1, D_INNER)),
            _const_spec((1, D_INNER)),
            _const_spec((LANES, D_INNER)),
        ],
        out_specs=pl.BlockSpec((SSD_L, D_INNER), row),
        out_shape=jax.ShapeDtypeStruct((t, D_INNER), BF16),
        scratch_shapes=[pltpu.VMEM((SSM_STATE, D_INNER), F32)],
        compiler_params=_params(("parallel", "arbitrary")),
        name="ssd",
    )(xbc, z, dt, alog, dskip, nw, expand)


def _attn_weights(w_in, w_out, q_norm_a, k_norm_a, q_norm_b, k_norm_b):
    da = H_A * HEAD_DIM
    qa, ka, va, qb, kb, vb = jnp.split(w_in, [da, 2 * da, 3 * da, 4 * da, 4 * da + 128], axis=1)
    perm = np.concatenate([np.arange(HEAD_DIM) + (j + 4 * e) * HEAD_DIM
                           for j in range(H_B // 2) for e in range(2)])
    w = jnp.concatenate([qa, ka, qb[:, perm], va, kb, vb], axis=1).astype(BF16)
    scale = HEAD_DIM ** -0.5
    ones = jnp.ones((1,), F32)
    gain = jnp.concatenate([
        jnp.tile(q_norm_a, H_A) * scale, jnp.tile(k_norm_a, H_A), jnp.tile(q_norm_b, H_B) * scale,
        jnp.tile(ones, da), jnp.tile(k_norm_b, H_B_KV), jnp.tile(ones, 128)]).reshape(1, QKV_W)
    wo = jnp.concatenate([w_out[:da], w_out[da:][perm]], axis=0).astype(BF16)
    return w, gain.astype(F32), wo


def _attn_bias(relpos_table):
    neg = -jnp.inf
    r = np.arange(QBLK)[:, None]
    jj = np.arange(A_BAND)[None, :]
    dchunk = jj // CHUNK - r // CHUNK
    ok = (dchunk >= 0) & (dchunk <= A_PREV)
    tbl = relpos_table.astype(F32)
    n_clip = A_BAND - 1 - MAX_REL
    by_rel = jnp.concatenate(
        [tbl[:, MAX_REL - (QBLK - 1):2 * MAX_REL],
         jnp.broadcast_to(tbl[:, 2 * MAX_REL:], (H_A, n_clip + 1))], axis=1)
    p = A_BAND + QBLK
    rev = jnp.pad(by_rel[:, ::-1], ((0, 0), (0, 1)))
    toep = jnp.tile(rev, (1, QBLK))[:, :QBLK * (p - 1)].reshape(H_A, QBLK, p - 1)
    toep = toep[:, :, QBLK - 1:QBLK - 1 + A_BAND]
    bias_a = jnp.where(ok[None], toep, neg)
    jb = np.arange(B_BAND)[None, :]
    relb = r - jb + B_PREV * CHUNK
    dcb = jb // CHUNK - r // CHUNK
    okb = (dcb >= 0) & (dcb <= B_PREV)
    slopes = 2.0 ** (-8.0 * jnp.arange(1, H_B + 1, dtype=F32) / H_B)
    bias_b = jnp.where(okb[None], -slopes[:, None, None] * jnp.abs(relb).astype(F32)[None], neg)
    return bias_a, bias_b


def _block_diag_ones():
    i = np.arange(256)
    return jnp.asarray((i[:, None] // HEAD_DIM) == (i[None, :] // HEAD_DIM), dtype=BF16)


def _ssd_expand_matrix():
    e = np.zeros((LANES, D_INNER), np.float32)
    for part in range(3):
        for hd in range(SSM_HEADS):
            e[part * SSM_HEADS + hd, hd * HEAD_DIM:(hd + 1) * HEAD_DIM] = 1.0
    return jnp.asarray(e, dtype=BF16)


def kernel(x, norm_mix, norm_ffn, attn_w_in, attn_w_out, relpos_table, q_norm_a, k_norm_a, q_norm_b,
           k_norm_b, sinks, ssm_w_in, ssm_conv_w, ssm_conv_b, ssm_dt_bias, ssm_a_log, ssm_d, ssm_norm,
           ssm_w_out, ffn_w_in, ffn_conv_w, ffn_conv_b, ffn_w_out):
    batch, seq, _ = x.shape
    assert seq % TM_PROJ == 0 and seq % QBLK == 0 and seq % SSD_L == 0
    x2 = x.reshape(batch * seq, D_MODEL)
    row = lambda v: v.reshape(1, -1).astype(F32)

    def ffn(layer, mix, xin, wo):
        wg, wu = ffn_w_in[layer, :, :D_FF], ffn_w_in[layer, :, D_FF:]
        return _mix_ffn(mix, xin, wo, row(norm_ffn[layer]), wg.astype(BF16), wu.astype(BF16),
                        ffn_conv_w[layer].astype(F32), row(ffn_conv_b[layer]),
                        ffn_w_out[layer].astype(BF16), batch, seq)

    w, gain, wo = _attn_weights(attn_w_in[0], attn_w_out[0], q_norm_a[0], k_norm_a[0], q_norm_b[0],
                                k_norm_b[0])
    qkv = _attn_inproj(x2, row(norm_mix[0]), w, gain, _block_diag_ones())
    bias_a, bias_b = _attn_bias(relpos_table[0])
    heads = _attention(qkv, sinks[0].astype(F32), bias_a, bias_b, batch, seq)
    x2 = ffn(0, heads, x2, wo)

    pad = SSM_PROJ_W - ssm_w_in.shape[2]
    w_ssm = jnp.pad(ssm_w_in[0], ((0, 0), (0, pad))).astype(BF16)
    pad_h = LANES - SSM_HEADS
    dtb = jnp.pad(ssm_dt_bias[0], (0, pad_h)).reshape(1, LANES).astype(F32)
    alog = jnp.pad(ssm_a_log[0], (0, pad_h)).reshape(1, LANES).astype(F32)
    z, xbc, dt = _ssm_inproj(x2, row(norm_mix[1]), w_ssm, ssm_conv_w[0].astype(F32),
                             row(ssm_conv_b[0]), dtb, batch, seq)
    dskip = row(jnp.repeat(ssm_d[0], HEAD_DIM))
    y = _ssd(z, xbc, dt, alog, dskip, row(ssm_norm[0]), _ssd_expand_matrix(), batch, seq)
    x2 = ffn(1, y, x2, ssm_w_out[0].astype(BF16))
    return x2.reshape(batch, seq, D_MODEL)
```

```python
import functools

import jax
import jax.numpy as jnp
import numpy as np
from jax import lax
from jax.experimental import pallas as pl
from jax.experimental.pallas import tpu as pltpu

F32 = jnp.float32
BF16 = jnp.bfloat16

LANES = 128
SUBLANES = 8
VMEM_LIMIT_BYTES = 56 * 1024 * 1024

D_MODEL = 1024
EPS = 1e-6
CHUNK = 64
HEAD_DIM = 64
H_A = 8
A_PREV = 8
MAX_REL = 256
H_B = 8
H_B_KV = 2
B_PREV = 2
D_INNER = 2048
SSM_HEADS = 32
SSM_GROUPS = 4
SSM_STATE = 128
SSM_CONV = 4
GROUP_W = D_INNER // SSM_GROUPS
HEADS_PER_GROUP = SSM_HEADS // SSM_GROUPS
D_FF = 2816
FFN_CONV = 3

QBLK = 256
QCH = QBLK // CHUNK
A_BLKS = 3
A_BAND = A_BLKS * QBLK
PV_HEADS = 4
LOG2E = 1.4426950408889634
B_BAND = QBLK + B_PREV * CHUNK

QA0, KA0, QB0, VA0, KB0, VB0 = 0, 512, 1024, 1536, 2048, 2176
QKV_W = 2304
NORM_CHUNKS = ((0, 256), (256, 256), (512, 256), (768, 256), (1024, 256), (1280, 256), (KB0, 128))
COPY_CHUNKS = ((VA0, 512), (VB0, 128))

XBC_W = D_INNER + 2 * SSM_GROUPS * SSM_STATE
SSM_PROJ_W = D_INNER + XBC_W + LANES
SSD_L = 128
SSD_ROWS = 128

TM_PROJ = 512
TM_FFN = 512
FFN_ROWS = 256


def _const_spec(shape):
    zeros = (0,) * len(shape)
    return pl.BlockSpec(shape, lambda *_: zeros, pipeline_mode=pl.Buffered(1))


def _params(sem):
    return pltpu.CompilerParams(dimension_semantics=sem, vmem_limit_bytes=VMEM_LIMIT_BYTES)


def _rms_rows(xf, gain):
    ms = jnp.mean(xf * xf, axis=-1, keepdims=True)
    return xf * lax.rsqrt(ms + EPS) * gain


def _shift_rows(y, prev8, k):
    r = pltpu.roll(y, k, axis=0)
    row = lax.broadcasted_iota(jnp.int32, (SUBLANES, 1), 0)
    first = jnp.where(row < k, pltpu.roll(prev8, k, axis=0), r[:SUBLANES])
    return jnp.concatenate([first, r[SUBLANES:]], axis=0)


def _sigmoid(x):
    return 1.0 / (1.0 + jnp.exp(-x))


def _split3(x):
    hi = x.astype(BF16)
    r1 = x - hi.astype(F32)
    mid = r1.astype(BF16)
    lo = (r1 - mid.astype(F32)).astype(BF16)
    return hi, mid, lo


def _attn_inproj_kernel(x_ref, g_ref, w_ref, gain_ref, bd_ref, o_ref):
    h = _rms_rows(x_ref[...], g_ref[...]).astype(BF16)
    y = jnp.dot(h, w_ref[...], preferred_element_type=F32)
    for c0, w in NORM_CHUNKS:
        yc = y[:, c0:c0 + w]
        ss = jnp.dot((yc * yc).astype(BF16), bd_ref[:w, :w], preferred_element_type=F32)
        r = lax.rsqrt(ss * (1.0 / HEAD_DIM) + EPS)
        o_ref[:, c0:c0 + w] = (yc * r * gain_ref[:, c0:c0 + w]).astype(BF16)
    for c0, w in COPY_CHUNKS:
        o_ref[:, c0:c0 + w] = y[:, c0:c0 + w].astype(BF16)


def _attn_inproj(x2, g, w, gain, bd):
    t = x2.shape[0]
    return pl.pallas_call(
        _attn_inproj_kernel,
        grid=(t // TM_PROJ,),
        in_specs=[
            pl.BlockSpec((TM_PROJ, D_MODEL), lambda i: (i, 0)),
            _const_spec((1, D_MODEL)),
            _const_spec((D_MODEL, QKV_W)),
            _const_spec((1, QKV_W)),
            _const_spec((256, 256)),
        ],
        out_specs=pl.BlockSpec((TM_PROJ, QKV_W), lambda i: (i, 0)),
        out_shape=jax.ShapeDtypeStruct((t, QKV_W), BF16),
        compiler_params=_params(("parallel",)),
        name="attn_inproj",
    )(x2, g, w, gain, bd)


def _attention_kernel(sink_ref, qa_ref, ka2_ref, ka1_ref, ka0_ref, va2_ref, va1_ref, va0_ref,
                      qb_ref, kb1_ref, kb0_ref, vb1_ref, vb0_ref, biasa_ref, biasb_ref, o_ref):
    lane = lax.broadcasted_iota(jnp.int32, (1, LANES), 1)
    lo_half = lane < HEAD_DIM
    half_masks = (lo_half.astype(BF16), (~lo_half).astype(BF16))
    nt = (((1,), (1,)), ((), ()))

    quad_w = PV_HEADS * HEAD_DIM
    lane_q = lax.broadcasted_iota(jnp.int32, (1, quad_w), 1)
    for quad in range(H_A // PV_HEADS):
        ps, inv_l = [], []
        for pr in range(PV_HEADS // 2):
            sl = slice(quad * quad_w + pr * LANES, quad * quad_w + (pr + 1) * LANES)
            k = jnp.concatenate([ka2_ref[:, sl], ka1_ref[:, sl], ka0_ref[:, sl]], axis=0)
            q = qa_ref[:, sl]
            for e in range(2):
                hd = quad * PV_HEADS + 2 * pr + e
                s = lax.dot_general(q * half_masks[e], k, nt, preferred_element_type=F32) + biasa_ref[hd]
                p = jnp.exp2(s - jnp.max(s, axis=-1, keepdims=True))
                inv_l.append(1.0 / jnp.sum(p, axis=-1, keepdims=True))
                ps.append(p.astype(BF16))
        ql = slice(quad * quad_w, (quad + 1) * quad_w)
        v = jnp.concatenate([va2_ref[:, ql], va1_ref[:, ql], va0_ref[:, ql]], axis=0)
        out = jnp.dot(jnp.concatenate(ps, axis=0), v, preferred_element_type=F32)
        o = out[(PV_HEADS - 1) * QBLK:] * inv_l[PV_HEADS - 1]
        for hh in range(PV_HEADS - 2, -1, -1):
            o = jnp.where(lane_q < (hh + 1) * HEAD_DIM, out[hh * QBLK:(hh + 1) * QBLK] * inv_l[hh], o)
        o_ref[:, ql] = o.astype(BF16)

    kb = jnp.concatenate([kb1_ref[QBLK - B_PREV * CHUNK:, :], kb0_ref[...]], axis=0)
    vb = jnp.concatenate([vb1_ref[QBLK - B_PREV * CHUNK:, :], vb0_ref[...]], axis=0)
    ob0 = H_A * HEAD_DIM
    for j in range(H_B // 2):
        q = qb_ref[:, j * LANES:(j + 1) * LANES]
        outs = []
        for e in range(2):
            hd = j + (H_B // 2) * e
            snk = sink_ref[hd]
            s = lax.dot_general(q * half_masks[e], kb, nt, preferred_element_type=F32) + biasb_ref[hd]
            m = jnp.maximum(jnp.max(s, axis=-1, keepdims=True), snk)
            p = jnp.exp2(s - m)
            l = jnp.sum(p, axis=-1, keepdims=True) + jnp.exp2(snk - m)
            outs.append(jnp.dot(p.astype(BF16), vb, preferred_element_type=F32) / l)
        o_ref[:, ob0 + j * LANES:ob0 + (j + 1) * LANES] = jnp.where(lo_half, outs[0], outs[1]).astype(BF16)


def _attention(qkv, sinks, bias_a, bias_b, batch, seq):
    t = qkv.shape[0]
    nq = seq // QBLK

    def rows(d):
        return lambda b, i: b * nq + jnp.maximum(i - d, 0)

    def spec(width, col, d):
        r = rows(d)
        return pl.BlockSpec((QBLK, width), lambda b, i: (r(b, i), col))

    wa, wb = H_A * HEAD_DIM, H_B_KV * HEAD_DIM
    in_specs = [
        pl.BlockSpec(memory_space=pltpu.SMEM),
        spec(wa, QA0 // wa, 0),
        spec(wa, KA0 // wa, 2), spec(wa, KA0 // wa, 1), spec(wa, KA0 // wa, 0),
        spec(wa, VA0 // wa, 2), spec(wa, VA0 // wa, 1), spec(wa, VA0 // wa, 0),
        spec(wa, QB0 // wa, 0),
        spec(wb, KB0 // wb, 1), spec(wb, KB0 // wb, 0),
        spec(wb, VB0 // wb, 1), spec(wb, VB0 // wb, 0),
        pl.BlockSpec((None, H_A, QBLK, A_BAND), lambda b, i: (jnp.minimum(i, A_BLKS - 1), 0, 0, 0)),
        pl.BlockSpec((None, H_B, QBLK, B_BAND), lambda b, i: (jnp.minimum(i, 1), 0, 0, 0)),
    ]
    return pl.pallas_call(
        _attention_kernel,
        grid=(batch, nq),
        in_specs=in_specs,
        out_specs=pl.BlockSpec((QBLK, D_MODEL), lambda b, i: (b * nq + i, 0)),
        out_shape=jax.ShapeDtypeStruct((t, D_MODEL), BF16),
        compiler_params=_params(("parallel", "parallel")),
        name="attention",
    )(sinks, qkv, qkv, qkv, qkv, qkv, qkv, qkv, qkv, qkv, qkv, qkv, qkv, bias_a, bias_b)


def _mix_ffn_kernel(mix_ref, x_ref, wo_ref, gn_ref, wg_ref, wu_ref, cw_ref, cb_ref, wd_ref,
                    o_ref, carry_ref):
    @pl.when(pl.program_id(1) == 0)
    def _():
        carry_ref[...] = jnp.zeros_like(carry_ref)

    prev8 = carry_ref[...]
    for r in range(TM_FFN // FFN_ROWS):
        rs = slice(r * FFN_ROWS, (r + 1) * FFN_ROWS)
        x1 = x_ref[rs, :] + jnp.dot(mix_ref[rs, :], wo_ref[...], preferred_element_type=F32)
        h = _rms_rows(x1, gn_ref[...]).astype(BF16)
        g = jnp.dot(h, wg_ref[...], preferred_element_type=F32)
        u = jnp.dot(h, wu_ref[...], preferred_element_type=F32)
        gc = (cw_ref[0:1, :] * _shift_rows(g, prev8, 2) + cw_ref[1:2, :] * _shift_rows(g, prev8, 1)
              + cw_ref[2:3, :] * g + cb_ref[...])
        prev8 = g[FFN_ROWS - SUBLANES:]
        act = (gc * _sigmoid(gc) * u).astype(BF16)
        o_ref[rs, :] = x1 + jnp.dot(act, wd_ref[...], preferred_element_type=F32)
    carry_ref[...] = prev8


def _mix_ffn(mix, x2, wo, gn, wg, wu, cw, cb, wd, batch, seq):
    t, kmix = mix.shape
    nt = seq // TM_FFN
    row = lambda b, j: (b * nt + j, 0)
    return pl.pallas_call(
        _mix_ffn_kernel,
        grid=(batch, nt),
        in_specs=[
            pl.BlockSpec((TM_FFN, kmix), row),
            pl.BlockSpec((TM_FFN, D_MODEL), row),
            _const_spec((kmix, D_MODEL)),
            _const_spec((1, D_MODEL)),
            _const_spec((D_MODEL, D_FF)),
            _const_spec((D_MODEL, D_FF)),
            _const_spec((FFN_CONV, D_FF)),
            _const_spec((1, D_FF)),
            _const_spec((D_FF, D_MODEL)),
        ],
        out_specs=pl.BlockSpec((TM_FFN, D_MODEL), row),
        out_shape=jax.ShapeDtypeStruct((t, D_MODEL), F32),
        scratch_shapes=[pltpu.VMEM((SUBLANES, D_FF), F32)],
        compiler_params=_params(("parallel", "arbitrary")),
        name="mix_ffn",
    )(mix, x2, wo, gn, wg, wu, cw, cb, wd)


SSM_NCH = 512


def _ssm_inproj_kernel(x_ref, g_ref, w_ref, cw_ref, cb_ref, dtb_ref, z_ref, xbc_ref, dt_ref,
                       carry_ref):
    @pl.when(pl.program_id(1) == 0)
    def _():
        carry_ref[...] = jnp.zeros_like(carry_ref)

    h = _rms_rows(x_ref[...], g_ref[...]).astype(BF16)
    n_z = D_INNER // SSM_NCH
    for c in range(XBC_W // SSM_NCH):
        if c < n_z:
            cs = slice(c * SSM_NCH, (c + 1) * SSM_NCH)
            z_ref[:, cs] = jnp.dot(h, w_ref[:, cs], preferred_element_type=F32).astype(BF16)
        cs = slice(c * SSM_NCH, (c + 1) * SSM_NCH)
        y = jnp.dot(h, w_ref[:, D_INNER + c * SSM_NCH:D_INNER + (c + 1) * SSM_NCH],
                    preferred_element_type=F32)
        prev8 = carry_ref[:, cs]
        acc = cw_ref[SSM_CONV - 1:SSM_CONV, cs] * y + cb_ref[:, cs]
        for k in range(1, SSM_CONV):
            acc = acc + cw_ref[SSM_CONV - 1 - k:SSM_CONV - k, cs] * _shift_rows(y, prev8, k)
        carry_ref[:, cs] = y[y.shape[0] - SUBLANES:]
        xbc_ref[:, cs] = (acc * _sigmoid(acc)).astype(BF16)
    raw = jnp.dot(h, w_ref[:, D_INNER + XBC_W:], preferred_element_type=F32) + dtb_ref[...]
    dt_ref[...] = jnp.maximum(raw, 0.0) + jnp.log1p(jnp.exp(-jnp.abs(raw)))


def _ssm_inproj(x2, g, w, cw, cb, dtb, batch, seq):
    t = x2.shape[0]
    nt = seq // TM_PROJ
    row = lambda b, j: (b * nt + j, 0)
    return pl.pallas_call(
        _ssm_inproj_kernel,
        grid=(batch, nt),
        in_specs=[
            pl.BlockSpec((TM_PROJ, D_MODEL), row),
            _const_spec((1, D_MODEL)),
            _const_spec((D_MODEL, SSM_PROJ_W)),
            _const_spec((SSM_CONV, XBC_W)),
            _const_spec((1, XBC_W)),
            _const_spec((1, LANES)),
        ],
        out_specs=[
            pl.BlockSpec((TM_PROJ, D_INNER), row),
            pl.BlockSpec((TM_PROJ, XBC_W), row),
            pl.BlockSpec((TM_PROJ, LANES), row),
        ],
        out_shape=[
            jax.ShapeDtypeStruct((t, D_INNER), BF16),
            jax.ShapeDtypeStruct((t, XBC_W), BF16),
            jax.ShapeDtypeStruct((t, LANES), F32),
        ],
        scratch_shapes=[pltpu.VMEM((SUBLANES, XBC_W), F32)],
        compiler_params=_params(("parallel", "arbitrary")),
        name="ssm_inproj",
    )(x2, g, w, cw, cb, dtb)


def _cumsum_rows(a):
    n = a.shape[0]
    row = lax.broadcasted_iota(jnp.int32, a.shape, 0)
    d = 1
    while d < n:
        if d < SUBLANES:
            shifted = jnp.where(row < d, 0.0, pltpu.roll(a, d, axis=0))
        else:
            shifted = jnp.concatenate([jnp.zeros((d, a.shape[1]), F32), a[:n - d]], axis=0)
        a = a + shifted
        d *= 2
    return a


HEADS_PER_DOT = 4


def _ssd_chunk(r0, states, xbc_ref, z_ref, dt_ref, alog_ref, dskip_ref, nw_ref, exp_ref, o_ref):
    ll = SSD_L
    rows = slice(r0, r0 + ll)
    nt = (((1,), (1,)), ((), ()))
    dt = dt_ref[rows, :]
    acs = _cumsum_rows(dt * -jnp.exp(alog_ref[...])) * LOG2E
    acs_t = jnp.transpose(acs)
    dt_t = jnp.transpose(dt)
    last = acs[ll - 1:ll, :]
    ea = jnp.exp2(acs)

    lane = lax.broadcasted_iota(jnp.int32, (1, LANES), 1)
    small = jnp.concatenate([jnp.exp2(last - acs) * dt, jnp.broadcast_to(ea[ll - 1:ll, :], (SUBLANES, LANES))],
                            axis=0)
    small = jnp.where(lane < SSM_HEADS, small, 0.0)
    hi = small.astype(BF16).astype(F32)
    r1 = small - hi
    mid = r1.astype(BF16).astype(F32)
    lo = (r1 - mid).astype(BF16).astype(F32)
    packed = (hi + pltpu.roll(mid, SSM_HEADS, axis=1) + pltpu.roll(lo, 2 * SSM_HEADS, axis=1)).astype(BF16)
    big = jnp.dot(packed, exp_ref[...], preferred_element_type=F32)
    w_e = big[:ll]
    decay_e = big[ll:ll + 1]

    causal = (lax.broadcasted_iota(jnp.int32, (ll, ll), 0) >= lax.broadcasted_iota(jnp.int32, (ll, ll), 1))
    wd = HEADS_PER_DOT * HEAD_DIM
    lane_w = lax.broadcasted_iota(jnp.int32, (1, wd), 1)
    new_states = []
    for g in range(SSM_GROUPS):
        c0 = g * GROUP_W
        state = states[g]
        state_b = state.astype(BF16)
        bmat = xbc_ref[rows, D_INNER + g * SSM_STATE:D_INNER + (g + 1) * SSM_STATE]
        c1 = D_INNER + (SSM_GROUPS + g) * SSM_STATE
        cmat = xbc_ref[rows, c1:c1 + SSM_STATE]
        cb = jnp.where(causal, lax.dot_general(cmat, bmat, nt, preferred_element_type=F32), 0.0)
        cf = cmat.astype(F32)
        ys = []
        for part in range(HEADS_PER_GROUP // HEADS_PER_DOT):
            lhs = []
            for hh in range(HEADS_PER_DOT):
                hd = g * HEADS_PER_GROUP + part * HEADS_PER_DOT + hh
                seg = jnp.minimum(acs[:, hd:hd + 1] - acs_t[hd:hd + 1, :], 0.0)
                m = cb * jnp.exp2(seg) * dt_t[hd:hd + 1, :]
                cea = cf * ea[:, hd:hd + 1]
                lhs.append(jnp.concatenate([m.astype(BF16), cea.astype(BF16)], axis=1))
            rhs = jnp.concatenate([xbc_ref[rows, c0 + part * wd:c0 + (part + 1) * wd],
                                   state_b[:, part * wd:(part + 1) * wd]], axis=0)
            out = jnp.dot(jnp.concatenate(lhs, axis=0), rhs, preferred_element_type=F32)
            y = out[(HEADS_PER_DOT - 1) * ll:]
            for hh in range(HEADS_PER_DOT - 2, -1, -1):
                y = jnp.where(lane_w < (hh + 1) * HEAD_DIM, out[hh * ll:(hh + 1) * ll], y)
            ys.append(y)
        xf = xbc_ref[rows, c0:c0 + GROUP_W].astype(F32)
        y = jnp.concatenate(ys, axis=1) + dskip_ref[:, c0:c0 + GROUP_W] * xf
        zf = z_ref[rows, c0:c0 + GROUP_W].astype(F32)
        o_ref[rows, c0:c0 + GROUP_W] = _rms_rows(y * (zf * _sigmoid(zf)),
                                                 nw_ref[:, c0:c0 + GROUP_W]).astype(BF16)
        xw = (xf * w_e[:, c0:c0 + GROUP_W]).astype(BF16)
        new_states.append(state * decay_e[:, c0:c0 + GROUP_W] + lax.dot_general(
            bmat, xw, (((0,), (0,)), ((), ())), preferred_element_type=F32))
    return new_states


def _ssd_kernel(xbc_ref, z_ref, dt_ref, alog_ref, dskip_ref, nw_ref, exp_ref, o_ref, state_ref):
    @pl.when(pl.program_id(1) == 0)
    def _():
        state_ref[...] = jnp.zeros_like(state_ref)

    states = [state_ref[:, g * GROUP_W:(g + 1) * GROUP_W] for g in range(SSM_GROUPS)]
    for c in range(SSD_ROWS // SSD_L):
        states = _ssd_chunk(c * SSD_L, states, xbc_ref, z_ref, dt_ref, alog_ref, dskip_ref, nw_ref,
                            exp_ref, o_ref)
    for g in range(SSM_GROUPS):
        state_ref[:, g * GROUP_W:(g + 1) * GROUP_W] = states[g]


def _ssd(z, xbc, dt, alog, dskip, nw, expand, batch, seq):
    t = z.shape[0]
    nc = seq // SSD_ROWS
    row = lambda b, c: (b * nc + c, 0)
    return pl.pallas_call(
        _ssd_kernel,
        grid=(batch, nc),
        in_specs=[
            pl.BlockSpec((SSD_ROWS, XBC_W), row),
            pl.BlockSpec((SSD_ROWS, D_INNER), row),
            pl.BlockSpec((SSD_ROWS, LANES), row),
            _const_spec((1, LANES)),
            _const_spec((1, D_INNER)),
            _const_spec((1, D_INNER)),
            _const_spec((LANES, D_INNER)),
        ],
        out_specs=pl.BlockSpec((SSD_ROWS, D_INNER), row),
        out_shape=jax.ShapeDtypeStruct((t, D_INNER), BF16),
        scratch_shapes=[pltpu.VMEM((SSM_STATE, D_INNER), F32)],
        compiler_params=_params(("parallel", "arbitrary")),
        name="ssd",
    )(xbc, z, dt, alog, dskip, nw, expand)


def _attn_weights(w_in, w_out, q_norm_a, k_norm_a, q_norm_b, k_norm_b):
    da = H_A * HEAD_DIM
    qa, ka, va, qb, kb, vb = jnp.split(w_in, [da, 2 * da, 3 * da, 4 * da, 4 * da + 128], axis=1)
    perm = np.concatenate([np.arange(HEAD_DIM) + (j + 4 * e) * HEAD_DIM
                           for j in range(H_B // 2) for e in range(2)])
    w = jnp.concatenate([qa, ka, qb[:, perm], va, kb, vb], axis=1).astype(BF16)
    scale = HEAD_DIM ** -0.5 * LOG2E
    ones = jnp.ones((1,), F32)
    gain = jnp.concatenate([
        jnp.tile(q_norm_a, H_A) * scale, jnp.tile(k_norm_a, H_A), jnp.tile(q_norm_b, H_B) * scale,
        jnp.tile(ones, da), jnp.tile(k_norm_b, H_B_KV), jnp.tile(ones, 128)]).reshape(1, QKV_W)
    wo = jnp.concatenate([w_out[:da], w_out[da:][perm]], axis=0).astype(BF16)
    return w, gain.astype(F32), wo


def _attn_bias(relpos_table):
    neg = -jnp.inf
    r = np.arange(QBLK)[:, None]
    jj = np.arange(A_BAND)[None, :]
    dchunk = jj // CHUNK - r // CHUNK
    ok = (dchunk >= 0) & (dchunk <= A_PREV)
    tbl = relpos_table.astype(F32)
    n_clip = A_BAND - 1 - MAX_REL
    by_rel = jnp.concatenate(
        [tbl[:, MAX_REL - (QBLK - 1):2 * MAX_REL],
         jnp.broadcast_to(tbl[:, 2 * MAX_REL:], (H_A, n_clip + 1))], axis=1)
    p = A_BAND + QBLK
    rev = jnp.pad(by_rel[:, ::-1], ((0, 0), (0, 1)))
    toep = jnp.tile(rev, (1, QBLK))[:, :QBLK * (p - 1)].reshape(H_A, QBLK, p - 1)
    toep = toep[:, :, QBLK - 1:QBLK - 1 + A_BAND]
    pad_ok = np.stack([jj >= (A_BLKS - 1 - n) * QBLK for n in range(A_BLKS)])
    bias_a = jnp.where((ok[None] & pad_ok)[:, None], toep[None] * LOG2E, neg)
    jb = np.arange(B_BAND)[None, :]
    relb = r - jb + B_PREV * CHUNK
    dcb = jb // CHUNK - r // CHUNK
    okb = (dcb >= 0) & (dcb <= B_PREV)
    slopes = 2.0 ** (-8.0 * jnp.arange(1, H_B + 1, dtype=F32) / H_B)
    alibi = -slopes[:, None, None] * jnp.abs(relb).astype(F32)[None] * LOG2E
    pad_okb = np.stack([jb >= B_PREV * CHUNK, jb >= 0])
    bias_b = jnp.where((okb[None] & pad_okb)[:, None], alibi[None], neg)
    return bias_a, bias_b


def _block_diag_ones():
    i = np.arange(256)
    return jnp.asarray((i[:, None] // HEAD_DIM) == (i[None, :] // HEAD_DIM), dtype=BF16)


def _ssd_expand_matrix():
    e = np.zeros((LANES, D_INNER), np.float32)
    for part in range(3):
        for hd in range(SSM_HEADS):
            e[part * SSM_HEADS + hd, hd * HEAD_DIM:(hd + 1) * HEAD_DIM] = 1.0
    return jnp.asarray(e, dtype=BF16)


def kernel(x, norm_mix, norm_ffn, attn_w_in, attn_w_out, relpos_table, q_norm_a, k_norm_a, q_norm_b,
           k_norm_b, sinks, ssm_w_in, ssm_conv_w, ssm_conv_b, ssm_dt_bias, ssm_a_log, ssm_d, ssm_norm,
           ssm_w_out, ffn_w_in, ffn_conv_w, ffn_conv_b, ffn_w_out):
    batch, seq, _ = x.shape
    assert seq % TM_PROJ == 0 and seq % QBLK == 0 and seq % SSD_ROWS == 0 and seq % TM_FFN == 0
    x2 = x.reshape(batch * seq, D_MODEL)
    row = lambda v: v.reshape(1, -1).astype(F32)

    def ffn(layer, mix, xin, wo):
        wg, wu = ffn_w_in[layer, :, :D_FF], ffn_w_in[layer, :, D_FF:]
        return _mix_ffn(mix, xin, wo, row(norm_ffn[layer]), wg.astype(BF16), wu.astype(BF16),
                        ffn_conv_w[layer].astype(F32), row(ffn_conv_b[layer]),
                        ffn_w_out[layer].astype(BF16), batch, seq)

    w, gain, wo = _attn_weights(attn_w_in[0], attn_w_out[0], q_norm_a[0], k_norm_a[0], q_norm_b[0],
                                k_norm_b[0])
    qkv = _attn_inproj(x2, row(norm_mix[0]), w, gain, _block_diag_ones())
    bias_a, bias_b = _attn_bias(relpos_table[0])
    heads = _attention(qkv, sinks[0].astype(F32) * LOG2E, bias_a, bias_b, batch, seq)
    x2 = ffn(0, heads, x2, wo)

    pad = SSM_PROJ_W - ssm_w_in.shape[2]
    w_ssm = jnp.pad(ssm_w_in[0], ((0, 0), (0, pad))).astype(BF16)
    pad_h = LANES - SSM_HEADS
    dtb = jnp.pad(ssm_dt_bias[0], (0, pad_h)).reshape(1, LANES).astype(F32)
    alog = jnp.pad(ssm_a_log[0], (0, pad_h)).reshape(1, LANES).astype(F32)
    z, xbc, dt = _ssm_inproj(x2, row(norm_mix[1]), w_ssm, ssm_conv_w[0].astype(F32),
                             row(ssm_conv_b[0]), dtb, batch, seq)
    dskip = row(jnp.repeat(ssm_d[0], HEAD_DIM))
    y = _ssd(z, xbc, dt, alog, dskip, row(ssm_norm[0]), _ssd_expand_matrix(), batch, seq)
    x2 = ffn(1, y, x2, ssm_w_out[0].astype(BF16))
    return x2.reshape(batch, seq, D_MODEL)
```

```python
import functools

import jax
import jax.numpy as jnp
import numpy as np
from jax import lax
from jax.experimental import pallas as pl
from jax.experimental.pallas import tpu as pltpu

F32 = jnp.float32
BF16 = jnp.bfloat16

LANES = 128
SUBLANES = 8
VMEM_LIMIT_BYTES = 56 * 1024 * 1024

D_MODEL = 1024
EPS = 1e-6
CHUNK = 64
HEAD_DIM = 64
H_A = 8
A_PREV = 8
MAX_REL = 256
H_B = 8
H_B_KV = 2
B_PREV = 2
D_INNER = 2048
SSM_HEADS = 32
SSM_GROUPS = 4
SSM_STATE = 128
SSM_CONV = 4
GROUP_W = D_INNER // SSM_GROUPS
HEADS_PER_GROUP = SSM_HEADS // SSM_GROUPS
D_FF = 2816
FFN_CONV = 3

QBLK = 256
PV_HEADS = 4
LOG2E = 1.4426950408889634
SUBQ = 128
A_SUB_BAND = SUBQ + A_PREV * CHUNK
B_SUB_BAND = SUBQ + B_PREV * CHUNK
A_VARIANTS = A_PREV * CHUNK // SUBQ + 1
B_VARIANTS = B_PREV * CHUNK // SUBQ + 1

QA0, KA0, QB0, VA0, KB0, VB0 = 0, 512, 1024, 1536, 2048, 2176
QKV_W = 2304
NORM_CHUNKS = ((0, 256), (256, 256), (512, 256), (768, 256), (1024, 256), (1280, 256), (KB0, 128))
COPY_CHUNKS = ((VA0, 512), (VB0, 128))

XBC_W = D_INNER + 2 * SSM_GROUPS * SSM_STATE
SSM_PROJ_W = D_INNER + XBC_W + LANES
SSD_L = 128
SSD_ROWS = 128

TM_PROJ = 512
TM_FFN = 512
FFN_ROWS = 256


def _const_spec(shape):
    zeros = (0,) * len(shape)
    return pl.BlockSpec(shape, lambda *_: zeros, pipeline_mode=pl.Buffered(1))


def _params(sem):
    return pltpu.CompilerParams(dimension_semantics=sem, vmem_limit_bytes=VMEM_LIMIT_BYTES)


def _rms_rows(xf, gain):
    ms = jnp.mean(xf * xf, axis=-1, keepdims=True)
    return xf * lax.rsqrt(ms + EPS) * gain


def _shift_rows(y, prev8, k):
    r = pltpu.roll(y, k, axis=0)
    row = lax.broadcasted_iota(jnp.int32, (SUBLANES, 1), 0)
    first = jnp.where(row < k, pltpu.roll(prev8, k, axis=0), r[:SUBLANES])
    return jnp.concatenate([first, r[SUBLANES:]], axis=0)


def _sigmoid(x):
    return 1.0 / (1.0 + jnp.exp(-x))


def _attn_inproj_kernel(x_ref, g_ref, w_ref, gain_ref, bd_ref, o_ref):
    h = _rms_rows(x_ref[...], g_ref[...]).astype(BF16)
    y = jnp.dot(h, w_ref[...], preferred_element_type=F32)
    for c0, w in NORM_CHUNKS:
        yc = y[:, c0:c0 + w]
        ss = jnp.dot((yc * yc).astype(BF16), bd_ref[:w, :w], preferred_element_type=F32)
        r = lax.rsqrt(ss * (1.0 / HEAD_DIM) + EPS)
        o_ref[:, c0:c0 + w] = (yc * r * gain_ref[:, c0:c0 + w]).astype(BF16)
    for c0, w in COPY_CHUNKS:
        o_ref[:, c0:c0 + w] = y[:, c0:c0 + w].astype(BF16)


def _attn_inproj(x2, g, w, gain, bd):
    t = x2.shape[0]
    return pl.pallas_call(
        _attn_inproj_kernel,
        grid=(t // TM_PROJ,),
        in_specs=[
            pl.BlockSpec((TM_PROJ, D_MODEL), lambda i: (i, 0)),
            _const_spec((1, D_MODEL)),
            _const_spec((D_MODEL, QKV_W)),
            _const_spec((1, QKV_W)),
            _const_spec((256, 256)),
        ],
        out_specs=pl.BlockSpec((TM_PROJ, QKV_W), lambda i: (i, 0)),
        out_shape=jax.ShapeDtypeStruct((t, QKV_W), BF16),
        compiler_params=_params(("parallel",)),
        name="attn_inproj",
    )(x2, g, w, gain, bd)


def _attention_kernel(sink_ref, qa_ref, ka2_ref, ka1_ref, ka0_ref, va2_ref, va1_ref, va0_ref,
                      qb_ref, kb1_ref, kb0_ref, vb1_ref, vb0_ref, ba0_ref, ba1_ref, bb0_ref, bb1_ref, o_ref):
    lane = lax.broadcasted_iota(jnp.int32, (1, LANES), 1)
    lo_half = lane < HEAD_DIM
    half_masks = (lo_half.astype(BF16), (~lo_half).astype(BF16))
    nt = (((1,), (1,)), ((), ()))
    quad_w = PV_HEADS * HEAD_DIM
    lane_q = lax.broadcasted_iota(jnp.int32, (1, quad_w), 1)
    ob0 = H_A * HEAD_DIM

    def two_heads(q):
        return jnp.concatenate([q * half_masks[0], q * half_masks[1]], axis=0)

    for sub in range(QBLK // SUBQ):
        rs = slice(sub * SUBQ, (sub + 1) * SUBQ)
        biasa_ref = (ba0_ref, ba1_ref)[sub]
        biasb_ref = (bb0_ref, bb1_ref)[sub]

        ka = slice(sub * SUBQ, sub * SUBQ + A_SUB_BAND)
        for quad in range(H_A // PV_HEADS):
            ps, inv_l = [], []
            for pr in range(PV_HEADS // 2):
                sl = slice(quad * quad_w + pr * LANES, quad * quad_w + (pr + 1) * LANES)
                k = jnp.concatenate([ka2_ref[:, sl], ka1_ref[:, sl], ka0_ref[:, sl]], axis=0)[ka]
                s2 = lax.dot_general(two_heads(qa_ref[rs, sl]), k, nt, preferred_element_type=F32)
                for e in range(2):
                    s = s2[e * SUBQ:(e + 1) * SUBQ] + biasa_ref[quad * PV_HEADS + 2 * pr + e]
                    p = jnp.exp2(s - jnp.max(s, axis=-1, keepdims=True))
                    inv_l.append(1.0 / jnp.sum(p, axis=-1, keepdims=True))
                    ps.append(p.astype(BF16))
            ql = slice(quad * quad_w, (quad + 1) * quad_w)
            v = jnp.concatenate([va2_ref[:, ql], va1_ref[:, ql], va0_ref[:, ql]], axis=0)[ka]
            out = jnp.dot(jnp.concatenate(ps, axis=0), v, preferred_element_type=F32)
            o = out[(PV_HEADS - 1) * SUBQ:] * inv_l[PV_HEADS - 1]
            for hh in range(PV_HEADS - 2, -1, -1):
                o = jnp.where(lane_q < (hh + 1) * HEAD_DIM, out[hh * SUBQ:(hh + 1) * SUBQ] * inv_l[hh], o)
            o_ref[rs, ql] = o.astype(BF16)

        if sub == 0:
            kb = jnp.concatenate([kb1_ref[QBLK - B_PREV * CHUNK:, :], kb0_ref[:SUBQ, :]], axis=0)
            vb = jnp.concatenate([vb1_ref[QBLK - B_PREV * CHUNK:, :], vb0_ref[:SUBQ, :]], axis=0)
        else:
            kb = kb0_ref[sub * SUBQ - B_PREV * CHUNK:(sub + 1) * SUBQ, :]
            vb = vb0_ref[sub * SUBQ - B_PREV * CHUNK:(sub + 1) * SUBQ, :]
        ps, inv_l = [], []
        for j in range(H_B // 2):
            s2 = lax.dot_general(two_heads(qb_ref[rs, j * LANES:(j + 1) * LANES]), kb, nt,
                                 preferred_element_type=F32)
            for e in range(2):
                hd = j + (H_B // 2) * e
                snk = sink_ref[hd]
                s = s2[e * SUBQ:(e + 1) * SUBQ] + biasb_ref[hd]
                m = jnp.maximum(jnp.max(s, axis=-1, keepdims=True), snk)
                p = jnp.exp2(s - m)
                inv_l.append(1.0 / (jnp.sum(p, axis=-1, keepdims=True) + jnp.exp2(snk - m)))
                ps.append(p.astype(BF16))
        out = jnp.dot(jnp.concatenate(ps, axis=0), vb, preferred_element_type=F32)
        for j in range(H_B // 2):
            o0 = out[(2 * j) * SUBQ:(2 * j + 1) * SUBQ] * inv_l[2 * j]
            o1 = out[(2 * j + 1) * SUBQ:(2 * j + 2) * SUBQ] * inv_l[2 * j + 1]
            o_ref[rs, ob0 + j * LANES:ob0 + (j + 1) * LANES] = jnp.where(lo_half, o0, o1).astype(BF16)


def _attention(qkv, sinks, bias_a, bias_b, batch, seq):
    t = qkv.shape[0]
    nq = seq // QBLK

    def rows(d):
        return lambda b, i: b * nq + jnp.maximum(i - d, 0)

    def spec(width, col, d):
        r = rows(d)
        return pl.BlockSpec((QBLK, width), lambda b, i: (r(b, i), col))

    wa, wb = H_A * HEAD_DIM, H_B_KV * HEAD_DIM
    in_specs = [
        pl.BlockSpec(memory_space=pltpu.SMEM),
        spec(wa, QA0 // wa, 0),
        spec(wa, KA0 // wa, 2), spec(wa, KA0 // wa, 1), spec(wa, KA0 // wa, 0),
        spec(wa, VA0 // wa, 2), spec(wa, VA0 // wa, 1), spec(wa, VA0 // wa, 0),
        spec(wa, QB0 // wa, 0),
        spec(wb, KB0 // wb, 1), spec(wb, KB0 // wb, 0),
        spec(wb, VB0 // wb, 1), spec(wb, VB0 // wb, 0),
    ]
    n_sub = QBLK // SUBQ
    for shape, n_var in (((None, H_A, SUBQ, A_SUB_BAND), A_VARIANTS), ((None, H_B, SUBQ, B_SUB_BAND), B_VARIANTS)):
        for sub in range(n_sub):
            in_specs.append(pl.BlockSpec(
                shape, lambda b, i, sub=sub, n_var=n_var: (jnp.minimum(n_sub * i + sub, n_var - 1), 0, 0, 0)))
    return pl.pallas_call(
        _attention_kernel,
        grid=(batch, nq),
        in_specs=in_specs,
        out_specs=pl.BlockSpec((QBLK, D_MODEL), lambda b, i: (b * nq + i, 0)),
        out_shape=jax.ShapeDtypeStruct((t, D_MODEL), BF16),
        compiler_params=_params(("parallel", "parallel")),
        name="attention",
    )(sinks, qkv, qkv, qkv, qkv, qkv, qkv, qkv, qkv, qkv, qkv, qkv, qkv, bias_a, bias_a, bias_b, bias_b)


def _mix_ffn_kernel(mix_ref, x_ref, wo_ref, gn_ref, win_ref, cw_ref, cb_ref, wd_ref, o_ref, carry_ref):
    @pl.when(pl.program_id(1) == 0)
    def _():
        carry_ref[...] = jnp.zeros_like(carry_ref)

    prev8 = carry_ref[...]
    for r in range(TM_FFN // FFN_ROWS):
        rs = slice(r * FFN_ROWS, (r + 1) * FFN_ROWS)
        x1 = x_ref[rs, :] + jnp.dot(mix_ref[rs, :], wo_ref[...], preferred_element_type=F32)
        h = _rms_rows(x1, gn_ref[...]).astype(BF16)
        g = jnp.dot(h, win_ref[:, :D_FF], preferred_element_type=F32)
        u = jnp.dot(h, win_ref[:, D_FF:], preferred_element_type=F32)
        gc = (cw_ref[0:1, :] * _shift_rows(g, prev8, 2) + cw_ref[1:2, :] * _shift_rows(g, prev8, 1)
              + cw_ref[2:3, :] * g + cb_ref[...])
        prev8 = g[FFN_ROWS - SUBLANES:]
        act = (gc * _sigmoid(gc) * u).astype(BF16)
        o_ref[rs, :] = x1 + jnp.dot(act, wd_ref[...], preferred_element_type=F32)
    carry_ref[...] = prev8


def _mix_ffn(mix, x2, wo, gn, w_in, cw, cb, wd, batch, seq):
    t, kmix = mix.shape
    nt = seq // TM_FFN
    row = lambda b, j: (b * nt + j, 0)
    return pl.pallas_call(
        _mix_ffn_kernel,
        grid=(batch, nt),
        in_specs=[
            pl.BlockSpec((TM_FFN, kmix), row),
            pl.BlockSpec((TM_FFN, D_MODEL), row),
            _const_spec((kmix, D_MODEL)),
            _const_spec((1, D_MODEL)),
            _const_spec((D_MODEL, 2 * D_FF)),
            _const_spec((FFN_CONV, D_FF)),
            _const_spec((1, D_FF)),
            _const_spec((D_FF, D_MODEL)),
        ],
        out_specs=pl.BlockSpec((TM_FFN, D_MODEL), row),
        out_shape=jax.ShapeDtypeStruct((t, D_MODEL), F32),
        scratch_shapes=[pltpu.VMEM((SUBLANES, D_FF), F32)],
        compiler_params=_params(("parallel", "arbitrary")),
        name="mix_ffn",
    )(mix, x2, wo, gn, w_in, cw, cb, wd)


SSM_NCH = 512


def _ssm_inproj_kernel(x_ref, g_ref, w_ref, cw_ref, cb_ref, dtb_ref, z_ref, xbc_ref, dt_ref,
                       carry_ref):
    @pl.when(pl.program_id(1) == 0)
    def _():
        carry_ref[...] = jnp.zeros_like(carry_ref)

    h = _rms_rows(x_ref[...], g_ref[...]).astype(BF16)
    n_z = D_INNER // SSM_NCH
    for c in range(XBC_W // SSM_NCH):
        if c < n_z:
            cs = slice(c * SSM_NCH, (c + 1) * SSM_NCH)
            z_ref[:, cs] = jnp.dot(h, w_ref[:, cs], preferred_element_type=F32).astype(BF16)
        cs = slice(c * SSM_NCH, (c + 1) * SSM_NCH)
        y = jnp.dot(h, w_ref[:, D_INNER + c * SSM_NCH:D_INNER + (c + 1) * SSM_NCH],
                    preferred_element_type=F32)
        prev8 = carry_ref[:, cs]
        acc = cw_ref[SSM_CONV - 1:SSM_CONV, cs] * y + cb_ref[:, cs]
        for k in range(1, SSM_CONV):
            acc = acc + cw_ref[SSM_CONV - 1 - k:SSM_CONV - k, cs] * _shift_rows(y, prev8, k)
        carry_ref[:, cs] = y[y.shape[0] - SUBLANES:]
        xbc_ref[:, cs] = (acc * _sigmoid(acc)).astype(BF16)
    raw = jnp.dot(h, w_ref[:, D_INNER + XBC_W:], preferred_element_type=F32) + dtb_ref[...]
    dt_ref[...] = jnp.maximum(raw, 0.0) + jnp.log1p(jnp.exp(-jnp.abs(raw)))


def _ssm_inproj(x2, g, w, cw, cb, dtb, batch, seq):
    t = x2.shape[0]
    nt = seq // TM_PROJ
    row = lambda b, j: (b * nt + j, 0)
    return pl.pallas_call(
        _ssm_inproj_kernel,
        grid=(batch, nt),
        in_specs=[
            pl.BlockSpec((TM_PROJ, D_MODEL), row),
            _const_spec((1, D_MODEL)),
            _const_spec((D_MODEL, SSM_PROJ_W)),
            _const_spec((SSM_CONV, XBC_W)),
            _const_spec((1, XBC_W)),
            _const_spec((1, LANES)),
        ],
        out_specs=[
            pl.BlockSpec((TM_PROJ, D_INNER), row),
            pl.BlockSpec((TM_PROJ, XBC_W), row),
            pl.BlockSpec((TM_PROJ, LANES), row),
        ],
        out_shape=[
            jax.ShapeDtypeStruct((t, D_INNER), BF16),
            jax.ShapeDtypeStruct((t, XBC_W), BF16),
            jax.ShapeDtypeStruct((t, LANES), F32),
        ],
        scratch_shapes=[pltpu.VMEM((SUBLANES, XBC_W), F32)],
        compiler_params=_params(("parallel", "arbitrary")),
        name="ssm_inproj",
    )(x2, g, w, cw, cb, dtb)


def _cumsum_rows(a):
    n = a.shape[0]
    row = lax.broadcasted_iota(jnp.int32, a.shape, 0)
    d = 1
    while d < n:
        if d < SUBLANES:
            shifted = jnp.where(row < d, 0.0, pltpu.roll(a, d, axis=0))
        else:
            shifted = jnp.concatenate([jnp.zeros((d, a.shape[1]), F32), a[:n - d]], axis=0)
        a = a + shifted
        d *= 2
    return a


HEADS_PER_DOT = 4


def _ssd_chunk(r0, states, xbc_ref, z_ref, dt_ref, alog_ref, dskip_ref, nw_ref, exp_ref, o_ref):
    ll = SSD_L
    rows = slice(r0, r0 + ll)
    nt = (((1,), (1,)), ((), ()))
    dt = dt_ref[rows, :]
    acs = _cumsum_rows(dt * -jnp.exp(alog_ref[...])) * LOG2E
    acs_t = jnp.transpose(acs)
    dt_t = jnp.transpose(dt)
    last = acs[ll - 1:ll, :]
    ea = jnp.exp2(acs)

    lane = lax.broadcasted_iota(jnp.int32, (1, LANES), 1)
    small = jnp.concatenate([jnp.exp2(last - acs) * dt, jnp.broadcast_to(ea[ll - 1:ll, :], (SUBLANES, LANES))],
                            axis=0)
    small = jnp.where(lane < SSM_HEADS, small, 0.0)
    hi = small.astype(BF16).astype(F32)
    r1 = small - hi
    mid = r1.astype(BF16).astype(F32)
    lo = (r1 - mid).astype(BF16).astype(F32)
    packed = (hi + pltpu.roll(mid, SSM_HEADS, axis=1) + pltpu.roll(lo, 2 * SSM_HEADS, axis=1)).astype(BF16)
    big = jnp.dot(packed, exp_ref[...], preferred_element_type=F32)
    w_e = big[:ll]
    decay_e = big[ll:ll + 1]

    causal = (lax.broadcasted_iota(jnp.int32, (ll, ll), 0) >= lax.broadcasted_iota(jnp.int32, (ll, ll), 1))
    wd = HEADS_PER_DOT * HEAD_DIM
    lane_w = lax.broadcasted_iota(jnp.int32, (1, wd), 1)
    new_states = []
    for g in range(SSM_GROUPS):
        c0 = g * GROUP_W
        state = states[g]
        state_b = state.astype(BF16)
        bmat = xbc_ref[rows, D_INNER + g * SSM_STATE:D_INNER + (g + 1) * SSM_STATE]
        c1 = D_INNER + (SSM_GROUPS + g) * SSM_STATE
        cmat = xbc_ref[rows, c1:c1 + SSM_STATE]
        cb = jnp.where(causal, lax.dot_general(cmat, bmat, nt, preferred_element_type=F32), 0.0)
        cf = cmat.astype(F32)
        ys = []
        for part in range(HEADS_PER_GROUP // HEADS_PER_DOT):
            lhs = []
            for hh in range(HEADS_PER_DOT):
                hd = g * HEADS_PER_GROUP + part * HEADS_PER_DOT + hh
                seg = jnp.minimum(acs[:, hd:hd + 1] - acs_t[hd:hd + 1, :], 0.0)
                m = cb * jnp.exp2(seg) * dt_t[hd:hd + 1, :]
                cea = cf * ea[:, hd:hd + 1]
                lhs.append(jnp.concatenate([m.astype(BF16), cea.astype(BF16)], axis=1))
            rhs = jnp.concatenate([xbc_ref[rows, c0 + part * wd:c0 + (part + 1) * wd],
                                   state_b[:, part * wd:(part + 1) * wd]], axis=0)
            out = jnp.dot(jnp.concatenate(lhs, axis=0), rhs, preferred_element_type=F32)
            y = out[(HEADS_PER_DOT - 1) * ll:]
            for hh in range(HEADS_PER_DOT - 2, -1, -1):
                y = jnp.where(lane_w < (hh + 1) * HEAD_DIM, out[hh * ll:(hh + 1) * ll], y)
            ys.append(y)
        xf = xbc_ref[rows, c0:c0 + GROUP_W].astype(F32)
        y = jnp.concatenate(ys, axis=1) + dskip_ref[:, c0:c0 + GROUP_W] * xf
        zf = z_ref[rows, c0:c0 + GROUP_W].astype(F32)
        o_ref[rows, c0:c0 + GROUP_W] = _rms_rows(y * (zf * _sigmoid(zf)),
                                                 nw_ref[:, c0:c0 + GROUP_W]).astype(BF16)
        xw = (xf * w_e[:, c0:c0 + GROUP_W]).astype(BF16)
        new_states.append(state * decay_e[:, c0:c0 + GROUP_W] + lax.dot_general(
            bmat, xw, (((0,), (0,)), ((), ())), preferred_element_type=F32))
    return new_states


def _ssd_kernel(xbc_ref, z_ref, dt_ref, alog_ref, dskip_ref, nw_ref, exp_ref, o_ref, state_ref):
    @pl.when(pl.program_id(1) == 0)
    def _():
        state_ref[...] = jnp.zeros_like(state_ref)

    states = [state_ref[:, g * GROUP_W:(g + 1) * GROUP_W] for g in range(SSM_GROUPS)]
    for c in range(SSD_ROWS // SSD_L):
        states = _ssd_chunk(c * SSD_L, states, xbc_ref, z_ref, dt_ref, alog_ref, dskip_ref, nw_ref,
                            exp_ref, o_ref)
    for g in range(SSM_GROUPS):
        state_ref[:, g * GROUP_W:(g + 1) * GROUP_W] = states[g]


def _ssd(z, xbc, dt, alog, dskip, nw, expand, batch, seq):
    t = z.shape[0]
    nc = seq // SSD_ROWS
    row = lambda b, c: (b * nc + c, 0)
    return pl.pallas_call(
        _ssd_kernel,
        grid=(batch, nc),
        in_specs=[
            pl.BlockSpec((SSD_ROWS, XBC_W), row),
            pl.BlockSpec((SSD_ROWS, D_INNER), row),
            pl.BlockSpec((SSD_ROWS, LANES), row),
            _const_spec((1, LANES)),
            _const_spec((1, D_INNER)),
            _const_spec((1, D_INNER)),
            _const_spec((LANES, D_INNER)),
        ],
        out_specs=pl.BlockSpec((SSD_ROWS, D_INNER), row),
        out_shape=jax.ShapeDtypeStruct((t, D_INNER), BF16),
        scratch_shapes=[pltpu.VMEM((SSM_STATE, D_INNER), F32)],
        compiler_params=_params(("parallel", "arbitrary")),
        name="ssd",
    )(xbc, z, dt, alog, dskip, nw, expand)


def _attn_weights(w_in, w_out, q_norm_a, k_norm_a, q_norm_b, k_norm_b):
    da = H_A * HEAD_DIM
    qa, ka, va, qb, kb, vb = jnp.split(w_in, [da, 2 * da, 3 * da, 4 * da, 4 * da + 128], axis=1)
    perm = np.concatenate([np.arange(HEAD_DIM) + (j + 4 * e) * HEAD_DIM
                           for j in range(H_B // 2) for e in range(2)])
    w = jnp.concatenate([qa, ka, qb[:, perm], va, kb, vb], axis=1).astype(BF16)
    scale = HEAD_DIM ** -0.5 * LOG2E
    ones = jnp.ones((1,), F32)
    gain = jnp.concatenate([
        jnp.tile(q_norm_a, H_A) * scale, jnp.tile(k_norm_a, H_A), jnp.tile(q_norm_b, H_B) * scale,
        jnp.tile(ones, da), jnp.tile(k_norm_b, H_B_KV), jnp.tile(ones, 128)]).reshape(1, QKV_W)
    wo = jnp.concatenate([w_out[:da], w_out[da:][perm]], axis=0).astype(BF16)
    return w, gain.astype(F32), wo


def _attn_bias(relpos_table):
    neg = -jnp.inf
    r = np.arange(SUBQ)[:, None]
    jj = np.arange(A_SUB_BAND)[None, :]
    dchunk = jj // CHUNK - r // CHUNK
    ok = (dchunk >= 0) & (dchunk <= A_PREV)
    tbl = relpos_table.astype(F32)
    n_clip = A_SUB_BAND - 1 - MAX_REL
    by_rel = jnp.concatenate(
        [tbl[:, MAX_REL - (SUBQ - 1):2 * MAX_REL],
         jnp.broadcast_to(tbl[:, 2 * MAX_REL:], (H_A, n_clip + 1))], axis=1)
    p = A_SUB_BAND + SUBQ
    rev = jnp.pad(by_rel[:, ::-1], ((0, 0), (0, 1)))
    toep = jnp.tile(rev, (1, SUBQ))[:, :SUBQ * (p - 1)].reshape(H_A, SUBQ, p - 1)
    toep = toep[:, :, SUBQ - 1:SUBQ - 1 + A_SUB_BAND]
    pad_ok = np.stack([jj >= A_PREV * CHUNK - n * SUBQ for n in range(A_VARIANTS)])
    bias_a = jnp.where((ok[None] & pad_ok)[:, None], toep[None] * LOG2E, neg)
    jb = np.arange(B_SUB_BAND)[None, :]
    relb = r - jb + B_PREV * CHUNK
    dcb = jb // CHUNK - r // CHUNK
    okb = (dcb >= 0) & (dcb <= B_PREV)
    slopes = 2.0 ** (-8.0 * jnp.arange(1, H_B + 1, dtype=F32) / H_B)
    alibi = -slopes[:, None, None] * jnp.abs(relb).astype(F32)[None] * LOG2E
    pad_okb = np.stack([jb >= B_PREV * CHUNK - n * SUBQ for n in range(B_VARIANTS)])
    bias_b = jnp.where((okb[None] & pad_okb)[:, None], alibi[None], neg)
    return bias_a, bias_b


def _block_diag_ones():
    i = np.arange(256)
    return jnp.asarray((i[:, None] // HEAD_DIM) == (i[None, :] // HEAD_DIM), dtype=BF16)


def _ssd_expand_matrix():
    e = np.zeros((LANES, D_INNER), np.float32)
    for part in range(3):
        for hd in range(SSM_HEADS):
            e[part * SSM_HEADS + hd, hd * HEAD_DIM:(hd + 1) * HEAD_DIM] = 1.0
    return jnp.asarray(e, dtype=BF16)


def kernel(x, norm_mix, norm_ffn, attn_w_in, attn_w_out, relpos_table, q_norm_a, k_norm_a, q_norm_b,
           k_norm_b, sinks, ssm_w_in, ssm_conv_w, ssm_conv_b, ssm_dt_bias, ssm_a_log, ssm_d, ssm_norm,
           ssm_w_out, ffn_w_in, ffn_conv_w, ffn_conv_b, ffn_w_out):
    batch, seq, _ = x.shape
    assert seq % TM_PROJ == 0 and seq % QBLK == 0 and seq % SSD_ROWS == 0 and seq % TM_FFN == 0
    x2 = x.reshape(batch * seq, D_MODEL)
    row = lambda v: v.reshape(1, -1).astype(F32)

    def ffn(layer, mix, xin, wo):
        return _mix_ffn(mix, xin, wo, row(norm_ffn[layer]), ffn_w_in[layer].astype(BF16),
                        ffn_conv_w[layer].astype(F32), row(ffn_conv_b[layer]),
                        ffn_w_out[layer].astype(BF16), batch, seq)

    w, gain, wo = _attn_weights(attn_w_in[0], attn_w_out[0], q_norm_a[0], k_norm_a[0], q_norm_b[0],
                                k_norm_b[0])
    qkv = _attn_inproj(x2, row(norm_mix[0]), w, gain, _block_diag_ones())
    bias_a, bias_b = _attn_bias(relpos_table[0])
    heads = _attention(qkv, sinks[0].astype(F32) * LOG2E, bias_a, bias_b, batch, seq)
    x2 = ffn(0, heads, x2, wo)

    pad = SSM_PROJ_W - ssm_w_in.shape[2]
    w_ssm = jnp.pad(ssm_w_in[0], ((0, 0), (0, pad))).astype(BF16)
    pad_h = LANES - SSM_HEADS
    dtb = jnp.pad(ssm_dt_bias[0], (0, pad_h)).reshape(1, LANES).astype(F32)
    alog = jnp.pad(ssm_a_log[0], (0, pad_h)).reshape(1, LANES).astype(F32)
    z, xbc, dt = _ssm_inproj(x2, row(norm_mix[1]), w_ssm, ssm_conv_w[0].astype(F32),
                             row(ssm_conv_b[0]), dtb, batch, seq)
    dskip = row(jnp.repeat(ssm_d[0], HEAD_DIM))
    y = _ssd(z, xbc, dt, alog, dskip, row(ssm_norm[0]), _ssd_expand_matrix(), batch, seq)
    x2 = ffn(1, y, x2, ssm_w_out[0].astype(BF16))
    return x2.reshape(batch, seq, D_MODEL)
```

```python
import functools

import jax
import jax.numpy as jnp
import numpy as np
from jax import lax
from jax.experimental import pallas as pl
from jax.experimental.pallas import tpu as pltpu

F32 = jnp.float32
BF16 = jnp.bfloat16

LANES = 128
SUBLANES = 8
VMEM_LIMIT_BYTES = 56 * 1024 * 1024

D_MODEL = 1024
EPS = 1e-6
CHUNK = 64
HEAD_DIM = 64
H_A = 8
A_PREV = 8
MAX_REL = 256
H_B = 8
H_B_KV = 2
B_PREV = 2
D_INNER = 2048
SSM_HEADS = 32
SSM_GROUPS = 4
SSM_STATE = 128
SSM_CONV = 4
GROUP_W = D_INNER // SSM_GROUPS
HEADS_PER_GROUP = SSM_HEADS // SSM_GROUPS
D_FF = 2816
FFN_CONV = 3

QBLK = 256
PV_HEADS = 4
LOG2E = 1.4426950408889634
SUBQ = 128
A_SUB_BAND = SUBQ + A_PREV * CHUNK
B_SUB_BAND = SUBQ + B_PREV * CHUNK
A_VARIANTS = A_PREV * CHUNK // SUBQ + 1
B_VARIANTS = B_PREV * CHUNK // SUBQ + 1

QA0, KA0, QB0, VA0, KB0, VB0 = 0, 512, 1024, 1536, 2048, 2176
QKV_W = 2304
NORM_CHUNKS = ((0, 256), (256, 256), (512, 256), (768, 256), (1024, 256), (1280, 256), (KB0, 128))
COPY_CHUNKS = ((VA0, 512), (VB0, 128))

XBC_W = D_INNER + 2 * SSM_GROUPS * SSM_STATE
SSM_PROJ_W = D_INNER + XBC_W + LANES
SSD_L = 128
SSD_ROWS = 128

TM_PROJ = 512
TM_FFN = 512
FFN_ROWS = 256


def _const_spec(shape):
    zeros = (0,) * len(shape)
    return pl.BlockSpec(shape, lambda *_: zeros, pipeline_mode=pl.Buffered(1))


def _params(sem):
    return pltpu.CompilerParams(dimension_semantics=sem, vmem_limit_bytes=VMEM_LIMIT_BYTES)


def _rms_rows(xf, gain):
    ms = jnp.mean(xf * xf, axis=-1, keepdims=True)
    return xf * lax.rsqrt(ms + EPS) * gain


def _shift_rows(y, prev8, k):
    r = pltpu.roll(y, k, axis=0)
    row = lax.broadcasted_iota(jnp.int32, (SUBLANES, 1), 0)
    first = jnp.where(row < k, pltpu.roll(prev8, k, axis=0), r[:SUBLANES])
    return jnp.concatenate([first, r[SUBLANES:]], axis=0)


def _sigmoid(x):
    return 1.0 / (1.0 + jnp.exp(-x))


def _attn_inproj_kernel(x_ref, g_ref, w_ref, gain_ref, bd_ref, o_ref):
    h = _rms_rows(x_ref[...], g_ref[...]).astype(BF16)
    y = jnp.dot(h, w_ref[...], preferred_element_type=F32)
    for c0, w in NORM_CHUNKS:
        yc = y[:, c0:c0 + w]
        ss = jnp.dot((yc * yc).astype(BF16), bd_ref[:w, :w], preferred_element_type=F32)
        r = lax.rsqrt(ss * (1.0 / HEAD_DIM) + EPS)
        o_ref[:, c0:c0 + w] = (yc * r * gain_ref[:, c0:c0 + w]).astype(BF16)
    for c0, w in COPY_CHUNKS:
        o_ref[:, c0:c0 + w] = y[:, c0:c0 + w].astype(BF16)


def _attn_inproj(x2, g, w, gain, bd):
    t = x2.shape[0]
    return pl.pallas_call(
        _attn_inproj_kernel,
        grid=(t // TM_PROJ,),
        in_specs=[
            pl.BlockSpec((TM_PROJ, D_MODEL), lambda i: (i, 0)),
            _const_spec((1, D_MODEL)),
            _const_spec((D_MODEL, QKV_W)),
            _const_spec((1, QKV_W)),
            _const_spec((256, 256)),
        ],
        out_specs=pl.BlockSpec((TM_PROJ, QKV_W), lambda i: (i, 0)),
        out_shape=jax.ShapeDtypeStruct((t, QKV_W), BF16),
        compiler_params=_params(("parallel",)),
        name="attn_inproj",
    )(x2, g, w, gain, bd)


def _attention_kernel(sink_ref, qa_ref, ka2_ref, ka1_ref, ka0_ref, va2_ref, va1_ref, va0_ref,
                      qb_ref, kb1_ref, kb0_ref, vb1_ref, vb0_ref, ba0_ref, ba1_ref, bb0_ref, bb1_ref, o_ref):
    lane = lax.broadcasted_iota(jnp.int32, (1, LANES), 1)
    lo_half = lane < HEAD_DIM
    half_masks = (lo_half.astype(BF16), (~lo_half).astype(BF16))
    nt = (((1,), (1,)), ((), ()))
    quad_w = PV_HEADS * HEAD_DIM
    lane_q = lax.broadcasted_iota(jnp.int32, (1, quad_w), 1)
    ob0 = H_A * HEAD_DIM

    def two_heads(q):
        return jnp.concatenate([q * half_masks[0], q * half_masks[1]], axis=0)

    for sub in range(QBLK // SUBQ):
        rs = slice(sub * SUBQ, (sub + 1) * SUBQ)
        biasa_ref = (ba0_ref, ba1_ref)[sub]
        biasb_ref = (bb0_ref, bb1_ref)[sub]

        ka = slice(sub * SUBQ, sub * SUBQ + A_SUB_BAND)
        for quad in range(H_A // PV_HEADS):
            ps, inv_l = [], []
            for pr in range(PV_HEADS // 2):
                sl = slice(quad * quad_w + pr * LANES, quad * quad_w + (pr + 1) * LANES)
                k = jnp.concatenate([ka2_ref[:, sl], ka1_ref[:, sl], ka0_ref[:, sl]], axis=0)[ka]
                s2 = lax.dot_general(two_heads(qa_ref[rs, sl]), k, nt, preferred_element_type=F32)
                for e in range(2):
                    s = s2[e * SUBQ:(e + 1) * SUBQ] + biasa_ref[quad * PV_HEADS + 2 * pr + e]
                    p = jnp.exp2(s - jnp.max(s, axis=-1, keepdims=True))
                    inv_l.append(1.0 / jnp.sum(p, axis=-1, keepdims=True))
                    ps.append(p.astype(BF16))
            ql = slice(quad * quad_w, (quad + 1) * quad_w)
            v = jnp.concatenate([va2_ref[:, ql], va1_ref[:, ql], va0_ref[:, ql]], axis=0)[ka]
            out = jnp.dot(jnp.concatenate(ps, axis=0), v, preferred_element_type=F32)
            o = out[(PV_HEADS - 1) * SUBQ:] * inv_l[PV_HEADS - 1]
            for hh in range(PV_HEADS - 2, -1, -1):
                o = jnp.where(lane_q < (hh + 1) * HEAD_DIM, out[hh * SUBQ:(hh + 1) * SUBQ] * inv_l[hh], o)
            o_ref[rs, ql] = o.astype(BF16)

        if sub == 0:
            kb = jnp.concatenate([kb1_ref[QBLK - B_PREV * CHUNK:, :], kb0_ref[:SUBQ, :]], axis=0)
            vb = jnp.concatenate([vb1_ref[QBLK - B_PREV * CHUNK:, :], vb0_ref[:SUBQ, :]], axis=0)
        else:
            kb = kb0_ref[sub * SUBQ - B_PREV * CHUNK:(sub + 1) * SUBQ, :]
            vb = vb0_ref[sub * SUBQ - B_PREV * CHUNK:(sub + 1) * SUBQ, :]
        ps, inv_l = [], []
        for j in range(H_B // 2):
            s2 = lax.dot_general(two_heads(qb_ref[rs, j * LANES:(j + 1) * LANES]), kb, nt,
                                 preferred_element_type=F32)
            for e in range(2):
                hd = j + (H_B // 2) * e
                snk = sink_ref[hd]
                s = s2[e * SUBQ:(e + 1) * SUBQ] + biasb_ref[hd]
                m = jnp.maximum(jnp.max(s, axis=-1, keepdims=True), snk)
                p = jnp.exp2(s - m)
                inv_l.append(1.0 / (jnp.sum(p, axis=-1, keepdims=True) + jnp.exp2(snk - m)))
                ps.append(p.astype(BF16))
        out = jnp.dot(jnp.concatenate(ps, axis=0), vb, preferred_element_type=F32)
        for j in range(H_B // 2):
            o0 = out[(2 * j) * SUBQ:(2 * j + 1) * SUBQ] * inv_l[2 * j]
            o1 = out[(2 * j + 1) * SUBQ:(2 * j + 2) * SUBQ] * inv_l[2 * j + 1]
            o_ref[rs, ob0 + j * LANES:ob0 + (j + 1) * LANES] = jnp.where(lo_half, o0, o1).astype(BF16)


def _attention(qkv, sinks, bias_a, bias_b, batch, seq):
    t = qkv.shape[0]
    nq = seq // QBLK

    def rows(d):
        return lambda b, i: b * nq + jnp.maximum(i - d, 0)

    def spec(width, col, d):
        r = rows(d)
        return pl.BlockSpec((QBLK, width), lambda b, i: (r(b, i), col))

    wa, wb = H_A * HEAD_DIM, H_B_KV * HEAD_DIM
    in_specs = [
        pl.BlockSpec(memory_space=pltpu.SMEM),
        spec(wa, QA0 // wa, 0),
        spec(wa, KA0 // wa, 2), spec(wa, KA0 // wa, 1), spec(wa, KA0 // wa, 0),
        spec(wa, VA0 // wa, 2), spec(wa, VA0 // wa, 1), spec(wa, VA0 // wa, 0),
        spec(wa, QB0 // wa, 0),
        spec(wb, KB0 // wb, 1), spec(wb, KB0 // wb, 0),
        spec(wb, VB0 // wb, 1), spec(wb, VB0 // wb, 0),
    ]
    n_sub = QBLK // SUBQ
    for shape, n_var in (((None, H_A, SUBQ, A_SUB_BAND), A_VARIANTS), ((None, H_B, SUBQ, B_SUB_BAND), B_VARIANTS)):
        for sub in range(n_sub):
            in_specs.append(pl.BlockSpec(
                shape, lambda b, i, sub=sub, n_var=n_var: (jnp.minimum(n_sub * i + sub, n_var - 1), 0, 0, 0)))
    return pl.pallas_call(
        _attention_kernel,
        grid=(batch, nq),
        in_specs=in_specs,
        out_specs=pl.BlockSpec((QBLK, D_MODEL), lambda b, i: (b * nq + i, 0)),
        out_shape=jax.ShapeDtypeStruct((t, D_MODEL), BF16),
        compiler_params=_params(("parallel", "parallel")),
        name="attention",
    )(sinks, qkv, qkv, qkv, qkv, qkv, qkv, qkv, qkv, qkv, qkv, qkv, qkv, bias_a, bias_a, bias_b, bias_b)


def _gate_norm(y, z, nw_ref):
    gated = y.astype(F32) * (z * _sigmoid(z)).astype(F32)
    outs = []
    for g in range(SSM_GROUPS):
        cs = slice(g * GROUP_W, (g + 1) * GROUP_W)
        outs.append(_rms_rows(gated[:, cs], nw_ref[:, cs]).astype(BF16))
    return jnp.concatenate(outs, axis=1)


def _mix_ffn_kernel(*refs, gated):
    if gated:
        mix_ref, z_ref, nw_ref, x_ref, wo_ref, gn_ref, win_ref, cw_ref, cb_ref, wd_ref, o_ref, carry_ref = refs
    else:
        mix_ref, x_ref, wo_ref, gn_ref, win_ref, cw_ref, cb_ref, wd_ref, o_ref, carry_ref = refs

    @pl.when(pl.program_id(1) == 0)
    def _():
        carry_ref[...] = jnp.zeros_like(carry_ref)

    prev8 = carry_ref[...]
    for r in range(TM_FFN // FFN_ROWS):
        rs = slice(r * FFN_ROWS, (r + 1) * FFN_ROWS)
        mix = mix_ref[rs, :]
        if gated:
            mix = _gate_norm(mix, z_ref[rs, :], nw_ref)
        x1 = x_ref[rs, :] + jnp.dot(mix, wo_ref[...], preferred_element_type=F32)
        h = _rms_rows(x1, gn_ref[...]).astype(BF16)
        g = jnp.dot(h, win_ref[:, :D_FF], preferred_element_type=F32)
        u = jnp.dot(h, win_ref[:, D_FF:], preferred_element_type=F32)
        gc = (cw_ref[0:1, :] * _shift_rows(g, prev8, 2) + cw_ref[1:2, :] * _shift_rows(g, prev8, 1)
              + cw_ref[2:3, :] * g + cb_ref[...])
        prev8 = g[FFN_ROWS - SUBLANES:]
        act = (gc * _sigmoid(gc) * u).astype(BF16)
        o_ref[rs, :] = x1 + jnp.dot(act, wd_ref[...], preferred_element_type=F32)
    carry_ref[...] = prev8


def _mix_ffn(mix, x2, wo, gn, w_in, cw, cb, wd, batch, seq, gate=None):
    t, kmix = mix.shape
    nt = seq // TM_FFN
    row = lambda b, j: (b * nt + j, 0)
    gate_specs = [] if gate is None else [pl.BlockSpec((TM_FFN, kmix), row), _const_spec((1, kmix))]
    return pl.pallas_call(
        functools.partial(_mix_ffn_kernel, gated=gate is not None),
        grid=(batch, nt),
        in_specs=[
            pl.BlockSpec((TM_FFN, kmix), row),
            *gate_specs,
            pl.BlockSpec((TM_FFN, D_MODEL), row),
            _const_spec((kmix, D_MODEL)),
            _const_spec((1, D_MODEL)),
            _const_spec((D_MODEL, 2 * D_FF)),
            _const_spec((FFN_CONV, D_FF)),
            _const_spec((1, D_FF)),
            _const_spec((D_FF, D_MODEL)),
        ],
        out_specs=pl.BlockSpec((TM_FFN, D_MODEL), row),
        out_shape=jax.ShapeDtypeStruct((t, D_MODEL), F32),
        scratch_shapes=[pltpu.VMEM((SUBLANES, D_FF), F32)],
        compiler_params=_params(("parallel", "arbitrary")),
        name="mix_ffn",
    )(mix, *(() if gate is None else gate), x2, wo, gn, w_in, cw, cb, wd)


SSM_NCH = 512


def _ssm_inproj_kernel(x_ref, g_ref, w_ref, cw_ref, cb_ref, dtb_ref, z_ref, xbc_ref, dt_ref,
                       carry_ref):
    @pl.when(pl.program_id(1) == 0)
    def _():
        carry_ref[...] = jnp.zeros_like(carry_ref)

    h = _rms_rows(x_ref[...], g_ref[...]).astype(BF16)
    n_z = D_INNER // SSM_NCH
    for c in range(XBC_W // SSM_NCH):
        if c < n_z:
            cs = slice(c * SSM_NCH, (c + 1) * SSM_NCH)
            z_ref[:, cs] = jnp.dot(h, w_ref[:, cs], preferred_element_type=F32).astype(BF16)
        cs = slice(c * SSM_NCH, (c + 1) * SSM_NCH)
        y = jnp.dot(h, w_ref[:, D_INNER + c * SSM_NCH:D_INNER + (c + 1) * SSM_NCH],
                    preferred_element_type=F32)
        prev8 = carry_ref[:, cs]
        acc = cw_ref[SSM_CONV - 1:SSM_CONV, cs] * y + cb_ref[:, cs]
        for k in range(1, SSM_CONV):
            acc = acc + cw_ref[SSM_CONV - 1 - k:SSM_CONV - k, cs] * _shift_rows(y, prev8, k)
        carry_ref[:, cs] = y[y.shape[0] - SUBLANES:]
        ab = acc.astype(BF16)
        xbc_ref[:, cs] = ab * _sigmoid(ab)
    raw = jnp.dot(h, w_ref[:, D_INNER + XBC_W:], preferred_element_type=F32) + dtb_ref[...]
    dt_ref[...] = jnp.maximum(raw, 0.0) + jnp.log1p(jnp.exp(-jnp.abs(raw)))


def _ssm_inproj(x2, g, w, cw, cb, dtb, batch, seq):
    t = x2.shape[0]
    nt = seq // TM_PROJ
    row = lambda b, j: (b * nt + j, 0)
    return pl.pallas_call(
        _ssm_inproj_kernel,
        grid=(batch, nt),
        in_specs=[
            pl.BlockSpec((TM_PROJ, D_MODEL), row),
            _const_spec((1, D_MODEL)),
            _const_spec((D_MODEL, SSM_PROJ_W)),
            _const_spec((SSM_CONV, XBC_W)),
            _const_spec((1, XBC_W)),
            _const_spec((1, LANES)),
        ],
        out_specs=[
            pl.BlockSpec((TM_PROJ, D_INNER), row),
            pl.BlockSpec((TM_PROJ, XBC_W), row),
            pl.BlockSpec((TM_PROJ, LANES), row),
        ],
        out_shape=[
            jax.ShapeDtypeStruct((t, D_INNER), BF16),
            jax.ShapeDtypeStruct((t, XBC_W), BF16),
            jax.ShapeDtypeStruct((t, LANES), F32),
        ],
        scratch_shapes=[pltpu.VMEM((SUBLANES, XBC_W), F32)],
        compiler_params=_params(("parallel", "arbitrary")),
        name="ssm_inproj",
    )(x2, g, w, cw, cb, dtb)


def _cumsum_rows(a):
    n = a.shape[0]
    row = lax.broadcasted_iota(jnp.int32, a.shape, 0)
    d = 1
    while d < n:
        if d < SUBLANES:
            shifted = jnp.where(row < d, 0.0, pltpu.roll(a, d, axis=0))
        else:
            shifted = jnp.concatenate([jnp.zeros((d, a.shape[1]), F32), a[:n - d]], axis=0)
        a = a + shifted
        d *= 2
    return a


HEADS_PER_DOT = 4


def _ssd_chunk(r0, states, xbc_ref, dt_ref, alog_ref, dskip_ref, exp_ref, o_ref):
    ll = SSD_L
    rows = slice(r0, r0 + ll)
    nt = (((1,), (1,)), ((), ()))
    dt = dt_ref[rows, :]
    acs = _cumsum_rows(dt * -jnp.exp(alog_ref[...])) * LOG2E
    acs_t = jnp.transpose(acs)
    dt_t = jnp.transpose(dt)
    last = acs[ll - 1:ll, :]
    ea = jnp.exp2(acs)

    lane = lax.broadcasted_iota(jnp.int32, (1, LANES), 1)
    small = jnp.concatenate([jnp.exp2(last - acs) * dt, jnp.broadcast_to(ea[ll - 1:ll, :], (SUBLANES, LANES))],
                            axis=0)
    small = jnp.where(lane < SSM_HEADS, small, 0.0)
    hi = small.astype(BF16).astype(F32)
    r1 = small - hi
    mid = r1.astype(BF16).astype(F32)
    lo = (r1 - mid).astype(BF16).astype(F32)
    packed = (hi + pltpu.roll(mid, SSM_HEADS, axis=1) + pltpu.roll(lo, 2 * SSM_HEADS, axis=1)).astype(BF16)
    big = jnp.dot(packed, exp_ref[...], preferred_element_type=F32)
    w_e = big[:ll]
    decay_e = big[ll:ll + 1]

    causal = (lax.broadcasted_iota(jnp.int32, (ll, ll), 0) >= lax.broadcasted_iota(jnp.int32, (ll, ll), 1))
    wd = HEADS_PER_DOT * HEAD_DIM
    lane_w = lax.broadcasted_iota(jnp.int32, (1, wd), 1)
    new_states = []
    for g in range(SSM_GROUPS):
        c0 = g * GROUP_W
        state = states[g]
        state_b = state.astype(BF16)
        bmat = xbc_ref[rows, D_INNER + g * SSM_STATE:D_INNER + (g + 1) * SSM_STATE]
        c1 = D_INNER + (SSM_GROUPS + g) * SSM_STATE
        cmat = xbc_ref[rows, c1:c1 + SSM_STATE]
        cb = jnp.where(causal, lax.dot_general(cmat, bmat, nt, preferred_element_type=F32), 0.0)
        cf = cmat.astype(F32)
        ys = []
        for part in range(HEADS_PER_GROUP // HEADS_PER_DOT):
            lhs = []
            for hh in range(HEADS_PER_DOT):
                hd = g * HEADS_PER_GROUP + part * HEADS_PER_DOT + hh
                seg = jnp.minimum(acs[:, hd:hd + 1] - acs_t[hd:hd + 1, :], 0.0)
                m = cb * jnp.exp2(seg) * dt_t[hd:hd + 1, :]
                cea = cf * ea[:, hd:hd + 1]
                lhs.append(jnp.concatenate([m.astype(BF16), cea.astype(BF16)], axis=1))
            rhs = jnp.concatenate([xbc_ref[rows, c0 + part * wd:c0 + (part + 1) * wd],
                                   state_b[:, part * wd:(part + 1) * wd]], axis=0)
            out = jnp.dot(jnp.concatenate(lhs, axis=0), rhs, preferred_element_type=F32)
            y = out[(HEADS_PER_DOT - 1) * ll:]
            for hh in range(HEADS_PER_DOT - 2, -1, -1):
                y = jnp.where(lane_w < (hh + 1) * HEAD_DIM, out[hh * ll:(hh + 1) * ll], y)
            ys.append(y)
        xf = xbc_ref[rows, c0:c0 + GROUP_W].astype(F32)
        y = jnp.concatenate(ys, axis=1) + dskip_ref[:, c0:c0 + GROUP_W] * xf
        o_ref[rows, c0:c0 + GROUP_W] = y.astype(BF16)
        xw = (xf * w_e[:, c0:c0 + GROUP_W]).astype(BF16)
        new_states.append(state * decay_e[:, c0:c0 + GROUP_W] + lax.dot_general(
            bmat, xw, (((0,), (0,)), ((), ())), preferred_element_type=F32))
    return new_states


def _ssd_kernel(xbc_ref, dt_ref, alog_ref, dskip_ref, exp_ref, o_ref, state_ref):
    @pl.when(pl.program_id(1) == 0)
    def _():
        state_ref[...] = jnp.zeros_like(state_ref)

    states = [state_ref[:, g * GROUP_W:(g + 1) * GROUP_W] for g in range(SSM_GROUPS)]
    for c in range(SSD_ROWS // SSD_L):
        states = _ssd_chunk(c * SSD_L, states, xbc_ref, dt_ref, alog_ref, dskip_ref, exp_ref, o_ref)
    for g in range(SSM_GROUPS):
        state_ref[:, g * GROUP_W:(g + 1) * GROUP_W] = states[g]


def _ssd(xbc, dt, alog, dskip, expand, batch, seq):
    t = xbc.shape[0]
    nc = seq // SSD_ROWS
    row = lambda b, c: (b * nc + c, 0)
    return pl.pallas_call(
        _ssd_kernel,
        grid=(batch, nc),
        in_specs=[
            pl.BlockSpec((SSD_ROWS, XBC_W), row),
            pl.BlockSpec((SSD_ROWS, LANES), row),
            _const_spec((1, LANES)),
            _const_spec((1, D_INNER)),
            _const_spec((LANES, D_INNER)),
        ],
        out_specs=pl.BlockSpec((SSD_ROWS, D_INNER), row),
        out_shape=jax.ShapeDtypeStruct((t, D_INNER), BF16),
        scratch_shapes=[pltpu.VMEM((SSM_STATE, D_INNER), F32)],
        compiler_params=_params(("parallel", "arbitrary")),
        name="ssd",
    )(xbc, dt, alog, dskip, expand)


def _attn_weights(w_in, w_out, q_norm_a, k_norm_a, q_norm_b, k_norm_b):
    da = H_A * HEAD_DIM
    qa, ka, va, qb, kb, vb = jnp.split(w_in, [da, 2 * da, 3 * da, 4 * da, 4 * da + 128], axis=1)
    perm = np.concatenate([np.arange(HEAD_DIM) + (j + 4 * e) * HEAD_DIM
                           for j in range(H_B // 2) for e in range(2)])
    w = jnp.concatenate([qa, ka, qb[:, perm], va, kb, vb], axis=1).astype(BF16)
    scale = HEAD_DIM ** -0.5 * LOG2E
    ones = jnp.ones((1,), F32)
    gain = jnp.concatenate([
        jnp.tile(q_norm_a, H_A) * scale, jnp.tile(k_norm_a, H_A), jnp.tile(q_norm_b, H_B) * scale,
        jnp.tile(ones, da), jnp.tile(k_norm_b, H_B_KV), jnp.tile(ones, 128)]).reshape(1, QKV_W)
    wo = jnp.concatenate([w_out[:da], w_out[da:][perm]], axis=0).astype(BF16)
    return w, gain.astype(F32), wo


def _attn_bias(relpos_table):
    neg = -jnp.inf
    r = np.arange(SUBQ)[:, None]
    jj = np.arange(A_SUB_BAND)[None, :]
    dchunk = jj // CHUNK - r // CHUNK
    ok = (dchunk >= 0) & (dchunk <= A_PREV)
    tbl = relpos_table.astype(F32)
    n_clip = A_SUB_BAND - 1 - MAX_REL
    by_rel = jnp.concatenate(
        [tbl[:, MAX_REL - (SUBQ - 1):2 * MAX_REL],
         jnp.broadcast_to(tbl[:, 2 * MAX_REL:], (H_A, n_clip + 1))], axis=1)
    p = A_SUB_BAND + SUBQ
    rev = jnp.pad(by_rel[:, ::-1], ((0, 0), (0, 1)))
    toep = jnp.tile(rev, (1, SUBQ))[:, :SUBQ * (p - 1)].reshape(H_A, SUBQ, p - 1)
    toep = toep[:, :, SUBQ - 1:SUBQ - 1 + A_SUB_BAND]
    pad_ok = np.stack([jj >= A_PREV * CHUNK - n * SUBQ for n in range(A_VARIANTS)])
    bias_a = jnp.where((ok[None] & pad_ok)[:, None], toep[None] * LOG2E, neg)
    jb = np.arange(B_SUB_BAND)[None, :]
    relb = r - jb + B_PREV * CHUNK
    dcb = jb // CHUNK - r // CHUNK
    okb = (dcb >= 0) & (dcb <= B_PREV)
    slopes = 2.0 ** (-8.0 * jnp.arange(1, H_B + 1, dtype=F32) / H_B)
    alibi = -slopes[:, None, None] * jnp.abs(relb).astype(F32)[None] * LOG2E
    pad_okb = np.stack([jb >= B_PREV * CHUNK - n * SUBQ for n in range(B_VARIANTS)])
    bias_b = jnp.where((okb[None] & pad_okb)[:, None], alibi[None], neg)
    return bias_a, bias_b


def _block_diag_ones():
    i = np.arange(256)
    return jnp.asarray((i[:, None] // HEAD_DIM) == (i[None, :] // HEAD_DIM), dtype=BF16)


def _ssd_expand_matrix():
    e = np.zeros((LANES, D_INNER), np.float32)
    for part in range(3):
        for hd in range(SSM_HEADS):
            e[part * SSM_HEADS + hd, hd * HEAD_DIM:(hd + 1) * HEAD_DIM] = 1.0
    return jnp.asarray(e, dtype=BF16)


def kernel(x, norm_mix, norm_ffn, attn_w_in, attn_w_out, relpos_table, q_norm_a, k_norm_a, q_norm_b,
           k_norm_b, sinks, ssm_w_in, ssm_conv_w, ssm_conv_b, ssm_dt_bias, ssm_a_log, ssm_d, ssm_norm,
           ssm_w_out, ffn_w_in, ffn_conv_w, ffn_conv_b, ffn_w_out):
    batch, seq, _ = x.shape
    assert seq % TM_PROJ == 0 and seq % QBLK == 0 and seq % SSD_ROWS == 0 and seq % TM_FFN == 0
    x2 = x.reshape(batch * seq, D_MODEL)
    row = lambda v: v.reshape(1, -1).astype(F32)

    def ffn(layer, mix, xin, wo, gate=None):
        return _mix_ffn(mix, xin, wo, row(norm_ffn[layer]), ffn_w_in[layer].astype(BF16),
                        ffn_conv_w[layer].astype(F32), row(ffn_conv_b[layer]),
                        ffn_w_out[layer].astype(BF16), batch, seq, gate=gate)

    w, gain, wo = _attn_weights(attn_w_in[0], attn_w_out[0], q_norm_a[0], k_norm_a[0], q_norm_b[0],
                                k_norm_b[0])
    qkv = _attn_inproj(x2, row(norm_mix[0]), w, gain, _block_diag_ones())
    bias_a, bias_b = _attn_bias(relpos_table[0])
    heads = _attention(qkv, sinks[0].astype(F32) * LOG2E, bias_a, bias_b, batch, seq)
    x2 = ffn(0, heads, x2, wo)

    pad = SSM_PROJ_W - ssm_w_in.shape[2]
    w_ssm = jnp.pad(ssm_w_in[0], ((0, 0), (0, pad))).astype(BF16)
    pad_h = LANES - SSM_HEADS
    dtb = jnp.pad(ssm_dt_bias[0], (0, pad_h)).reshape(1, LANES).astype(F32)
    alog = jnp.pad(ssm_a_log[0], (0, pad_h)).reshape(1, LANES).astype(F32)
    z, xbc, dt = _ssm_inproj(x2, row(norm_mix[1]), w_ssm, ssm_conv_w[0].astype(F32),
                             row(ssm_conv_b[0]), dtb, batch, seq)
    dskip = row(jnp.repeat(ssm_d[0], HEAD_DIM))
    y = _ssd(xbc, dt, alog, dskip, _ssd_expand_matrix(), batch, seq)
    x2 = ffn(1, y, x2, ssm_w_out[0].astype(BF16), gate=(z, row(ssm_norm[0])))
    return x2.reshape(batch, seq, D_MODEL)
```

```python
import functools

import jax
import jax.numpy as jnp
import numpy as np
from jax import lax
from jax.experimental import pallas as pl
from jax.experimental.pallas import tpu as pltpu

F32 = jnp.float32
BF16 = jnp.bfloat16

LANES = 128
SUBLANES = 8
VMEM_LIMIT_BYTES = 56 * 1024 * 1024

D_MODEL = 1024
EPS = 1e-6
CHUNK = 64
HEAD_DIM = 64
H_A = 8
A_PREV = 8
MAX_REL = 256
H_B = 8
H_B_KV = 2
B_PREV = 2
D_INNER = 2048
SSM_HEADS = 32
SSM_GROUPS = 4
SSM_STATE = 128
SSM_CONV = 4
GROUP_W = D_INNER // SSM_GROUPS
HEADS_PER_GROUP = SSM_HEADS // SSM_GROUPS
D_FF = 2816
FFN_CONV = 3

QBLK = 256
PV_HEADS = 4
LOG2E = 1.4426950408889634
SUBQ = 128
A_SUB_BAND = SUBQ + A_PREV * CHUNK
B_SUB_BAND = SUBQ + B_PREV * CHUNK
A_VARIANTS = A_PREV * CHUNK // SUBQ + 1
B_VARIANTS = B_PREV * CHUNK // SUBQ + 1

QA0, KA0, QB0, VA0, KB0, VB0 = 0, 512, 1024, 1536, 2048, 2176
QKV_W = 2304
NORM_CHUNKS = ((0, 256), (256, 256), (512, 256), (768, 256), (1024, 256), (1280, 256), (KB0, 128))
COPY_CHUNKS = ((VA0, 512), (VB0, 128))

XBC_W = D_INNER + 2 * SSM_GROUPS * SSM_STATE
SSM_PROJ_W = D_INNER + XBC_W + LANES
SSD_L = 128
SSD_ROWS = 512

TM_PROJ = 1024
TM_FFN = 512
FFN_ROWS = 256


def _const_spec(shape):
    zeros = (0,) * len(shape)
    return pl.BlockSpec(shape, lambda *_: zeros, pipeline_mode=pl.Buffered(1))


def _params(sem):
    return pltpu.CompilerParams(dimension_semantics=sem, vmem_limit_bytes=VMEM_LIMIT_BYTES)


def _rms_rows(xf, gain):
    ms = jnp.mean(xf * xf, axis=-1, keepdims=True)
    return xf * lax.rsqrt(ms + EPS) * gain


def _shift_rows(y, prev8, k):
    r = pltpu.roll(y, k, axis=0)
    row = lax.broadcasted_iota(jnp.int32, (SUBLANES, 1), 0)
    first = jnp.where(row < k, pltpu.roll(prev8, k, axis=0), r[:SUBLANES])
    return jnp.concatenate([first, r[SUBLANES:]], axis=0)


def _sigmoid(x):
    return 1.0 / (1.0 + jnp.exp(-x))


def _attn_inproj_kernel(x_ref, g_ref, w_ref, gain_ref, bd_ref, o_ref):
    h = _rms_rows(x_ref[...], g_ref[...]).astype(BF16)
    y = jnp.dot(h, w_ref[...], preferred_element_type=F32)
    for c0, w in NORM_CHUNKS:
        yc = y[:, c0:c0 + w]
        ss = jnp.dot((yc * yc).astype(BF16), bd_ref[:w, :w], preferred_element_type=F32)
        r = lax.rsqrt(ss * (1.0 / HEAD_DIM) + EPS)
        o_ref[:, c0:c0 + w] = (yc * r * gain_ref[:, c0:c0 + w]).astype(BF16)
    for c0, w in COPY_CHUNKS:
        o_ref[:, c0:c0 + w] = y[:, c0:c0 + w].astype(BF16)


def _attn_inproj(x2, g, w, gain, bd):
    t = x2.shape[0]
    return pl.pallas_call(
        _attn_inproj_kernel,
        grid=(t // TM_PROJ,),
        in_specs=[
            pl.BlockSpec((TM_PROJ, D_MODEL), lambda i: (i, 0)),
            _const_spec((1, D_MODEL)),
            _const_spec((D_MODEL, QKV_W)),
            _const_spec((1, QKV_W)),
            _const_spec((256, 256)),
        ],
        out_specs=pl.BlockSpec((TM_PROJ, QKV_W), lambda i: (i, 0)),
        out_shape=jax.ShapeDtypeStruct((t, QKV_W), BF16),
        compiler_params=_params(("parallel",)),
        name="attn_inproj",
    )(x2, g, w, gain, bd)


def _attention_kernel(sink_ref, qa_ref, ka2_ref, ka1_ref, ka0_ref, va2_ref, va1_ref, va0_ref,
                      qb_ref, kb1_ref, kb0_ref, vb1_ref, vb0_ref, ba0_ref, ba1_ref, bb0_ref, bb1_ref, o_ref):
    lane = lax.broadcasted_iota(jnp.int32, (1, LANES), 1)
    lo_half = lane < HEAD_DIM
    half_masks = (lo_half.astype(BF16), (~lo_half).astype(BF16))
    nt = (((1,), (1,)), ((), ()))
    quad_w = PV_HEADS * HEAD_DIM
    lane_q = lax.broadcasted_iota(jnp.int32, (1, quad_w), 1)
    ob0 = H_A * HEAD_DIM

    def two_heads(q):
        return jnp.concatenate([q * half_masks[0], q * half_masks[1]], axis=0)

    for sub in range(QBLK // SUBQ):
        rs = slice(sub * SUBQ, (sub + 1) * SUBQ)
        biasa_ref = (ba0_ref, ba1_ref)[sub]
        biasb_ref = (bb0_ref, bb1_ref)[sub]

        ka = slice(sub * SUBQ, sub * SUBQ + A_SUB_BAND)
        for quad in range(H_A // PV_HEADS):
            ps, inv_l = [], []
            for pr in range(PV_HEADS // 2):
                sl = slice(quad * quad_w + pr * LANES, quad * quad_w + (pr + 1) * LANES)
                k = jnp.concatenate([ka2_ref[:, sl], ka1_ref[:, sl], ka0_ref[:, sl]], axis=0)[ka]
                s2 = lax.dot_general(two_heads(qa_ref[rs, sl]), k, nt, preferred_element_type=F32)
                for e in range(2):
                    s = s2[e * SUBQ:(e + 1) * SUBQ] + biasa_ref[quad * PV_HEADS + 2 * pr + e]
                    p = jnp.exp2(s - jnp.max(s, axis=-1, keepdims=True))
                    inv_l.append(1.0 / jnp.sum(p, axis=-1, keepdims=True))
                    ps.append(p.astype(BF16))
            ql = slice(quad * quad_w, (quad + 1) * quad_w)
            v = jnp.concatenate([va2_ref[:, ql], va1_ref[:, ql], va0_ref[:, ql]], axis=0)[ka]
            out = jnp.dot(jnp.concatenate(ps, axis=0), v, preferred_element_type=F32)
            o = out[(PV_HEADS - 1) * SUBQ:] * inv_l[PV_HEADS - 1]
            for hh in range(PV_HEADS - 2, -1, -1):
                o = jnp.where(lane_q < (hh + 1) * HEAD_DIM, out[hh * SUBQ:(hh + 1) * SUBQ] * inv_l[hh], o)
            o_ref[rs, ql] = o.astype(BF16)

        if sub == 0:
            kb = jnp.concatenate([kb1_ref[QBLK - B_PREV * CHUNK:, :], kb0_ref[:SUBQ, :]], axis=0)
            vb = jnp.concatenate([vb1_ref[QBLK - B_PREV * CHUNK:, :], vb0_ref[:SUBQ, :]], axis=0)
        else:
            kb = kb0_ref[sub * SUBQ - B_PREV * CHUNK:(sub + 1) * SUBQ, :]
            vb = vb0_ref[sub * SUBQ - B_PREV * CHUNK:(sub + 1) * SUBQ, :]
        ps, inv_l = [], []
        for j in range(H_B // 2):
            s2 = lax.dot_general(two_heads(qb_ref[rs, j * LANES:(j + 1) * LANES]), kb, nt,
                                 preferred_element_type=F32)
            for e in range(2):
                hd = j + (H_B // 2) * e
                snk = sink_ref[hd]
                s = s2[e * SUBQ:(e + 1) * SUBQ] + biasb_ref[hd]
                m = jnp.maximum(jnp.max(s, axis=-1, keepdims=True), snk)
                p = jnp.exp2(s - m)
                inv_l.append(1.0 / (jnp.sum(p, axis=-1, keepdims=True) + jnp.exp2(snk - m)))
                ps.append(p.astype(BF16))
        out = jnp.dot(jnp.concatenate(ps, axis=0), vb, preferred_element_type=F32)
        for j in range(H_B // 2):
            o0 = out[(2 * j) * SUBQ:(2 * j + 1) * SUBQ] * inv_l[2 * j]
            o1 = out[(2 * j + 1) * SUBQ:(2 * j + 2) * SUBQ] * inv_l[2 * j + 1]
            o_ref[rs, ob0 + j * LANES:ob0 + (j + 1) * LANES] = jnp.where(lo_half, o0, o1).astype(BF16)


def _attention(qkv, sinks, bias_a, bias_b, batch, seq):
    t = qkv.shape[0]
    nq = seq // QBLK

    def rows(d):
        return lambda b, i: b * nq + jnp.maximum(i - d, 0)

    def spec(width, col, d):
        r = rows(d)
        return pl.BlockSpec((QBLK, width), lambda b, i: (r(b, i), col))

    wa, wb = H_A * HEAD_DIM, H_B_KV * HEAD_DIM
    in_specs = [
        pl.BlockSpec(memory_space=pltpu.SMEM),
        spec(wa, QA0 // wa, 0),
        spec(wa, KA0 // wa, 2), spec(wa, KA0 // wa, 1), spec(wa, KA0 // wa, 0),
        spec(wa, VA0 // wa, 2), spec(wa, VA0 // wa, 1), spec(wa, VA0 // wa, 0),
        spec(wa, QB0 // wa, 0),
        spec(wb, KB0 // wb, 1), spec(wb, KB0 // wb, 0),
        spec(wb, VB0 // wb, 1), spec(wb, VB0 // wb, 0),
    ]
    n_sub = QBLK // SUBQ
    for shape, n_var in (((None, H_A, SUBQ, A_SUB_BAND), A_VARIANTS), ((None, H_B, SUBQ, B_SUB_BAND), B_VARIANTS)):
        for sub in range(n_sub):
            in_specs.append(pl.BlockSpec(
                shape, lambda b, i, sub=sub, n_var=n_var: (jnp.minimum(n_sub * i + sub, n_var - 1), 0, 0, 0)))
    return pl.pallas_call(
        _attention_kernel,
        grid=(batch, nq),
        in_specs=in_specs,
        out_specs=pl.BlockSpec((QBLK, D_MODEL), lambda b, i: (b * nq + i, 0)),
        out_shape=jax.ShapeDtypeStruct((t, D_MODEL), BF16),
        compiler_params=_params(("parallel", "parallel")),
        name="attention",
    )(sinks, qkv, qkv, qkv, qkv, qkv, qkv, qkv, qkv, qkv, qkv, qkv, qkv, bias_a, bias_a, bias_b, bias_b)


def _gate_norm(y, z, nw_ref):
    gated = y.astype(F32) * (z * _sigmoid(z)).astype(F32)
    outs = []
    for g in range(SSM_GROUPS):
        cs = slice(g * GROUP_W, (g + 1) * GROUP_W)
        outs.append(_rms_rows(gated[:, cs], nw_ref[:, cs]).astype(BF16))
    return jnp.concatenate(outs, axis=1)


def _mix_ffn_kernel(*refs, gated):
    if gated:
        mix_ref, z_ref, nw_ref, x_ref, wo_ref, gn_ref, win_ref, cw_ref, cb_ref, wd_ref, o_ref, carry_ref = refs
    else:
        mix_ref, x_ref, wo_ref, gn_ref, win_ref, cw_ref, cb_ref, wd_ref, o_ref, carry_ref = refs

    @pl.when(pl.program_id(1) == 0)
    def _():
        carry_ref[...] = jnp.zeros_like(carry_ref)

    prev8 = carry_ref[...]
    for r in range(TM_FFN // FFN_ROWS):
        rs = slice(r * FFN_ROWS, (r + 1) * FFN_ROWS)
        mix = mix_ref[rs, :]
        if gated:
            mix = _gate_norm(mix, z_ref[rs, :], nw_ref)
        x1 = x_ref[rs, :] + jnp.dot(mix, wo_ref[...], preferred_element_type=F32)
        h = _rms_rows(x1, gn_ref[...]).astype(BF16)
        g = jnp.dot(h, win_ref[:, :D_FF], preferred_element_type=F32)
        u = jnp.dot(h, win_ref[:, D_FF:], preferred_element_type=F32)
        gc = (cw_ref[0:1, :] * _shift_rows(g, prev8, 2) + cw_ref[1:2, :] * _shift_rows(g, prev8, 1)
              + cw_ref[2:3, :] * g + cb_ref[...])
        prev8 = g[FFN_ROWS - SUBLANES:]
        act = (gc * _sigmoid(gc) * u).astype(BF16)
        o_ref[rs, :] = x1 + jnp.dot(act, wd_ref[...], preferred_element_type=F32)
    carry_ref[...] = prev8


def _mix_ffn(mix, x2, wo, gn, w_in, cw, cb, wd, batch, seq, gate=None):
    t, kmix = mix.shape
    nt = seq // TM_FFN
    row = lambda b, j: (b * nt + j, 0)
    gate_specs = [] if gate is None else [pl.BlockSpec((TM_FFN, kmix), row), _const_spec((1, kmix))]
    return pl.pallas_call(
        functools.partial(_mix_ffn_kernel, gated=gate is not None),
        grid=(batch, nt),
        in_specs=[
            pl.BlockSpec((TM_FFN, kmix), row),
            *gate_specs,
            pl.BlockSpec((TM_FFN, D_MODEL), row),
            _const_spec((kmix, D_MODEL)),
            _const_spec((1, D_MODEL)),
            _const_spec((D_MODEL, 2 * D_FF)),
            _const_spec((FFN_CONV, D_FF)),
            _const_spec((1, D_FF)),
            _const_spec((D_FF, D_MODEL)),
        ],
        out_specs=pl.BlockSpec((TM_FFN, D_MODEL), row),
        out_shape=jax.ShapeDtypeStruct((t, D_MODEL), F32),
        scratch_shapes=[pltpu.VMEM((SUBLANES, D_FF), F32)],
        compiler_params=_params(("parallel", "arbitrary")),
        name="mix_ffn",
    )(mix, *(() if gate is None else gate), x2, wo, gn, w_in, cw, cb, wd)


SSM_NCH = 512


def _ssm_inproj_kernel(x_ref, g_ref, w_ref, cw_ref, cb_ref, dtb_ref, z_ref, xbc_ref, dt_ref,
                       carry_ref):
    @pl.when(pl.program_id(1) == 0)
    def _():
        carry_ref[...] = jnp.zeros_like(carry_ref)

    h = _rms_rows(x_ref[...], g_ref[...]).astype(BF16)
    n_z = D_INNER // SSM_NCH
    for c in range(XBC_W // SSM_NCH):
        if c < n_z:
            cs = slice(c * SSM_NCH, (c + 1) * SSM_NCH)
            z_ref[:, cs] = jnp.dot(h, w_ref[:, cs], preferred_element_type=F32).astype(BF16)
        cs = slice(c * SSM_NCH, (c + 1) * SSM_NCH)
        y = jnp.dot(h, w_ref[:, D_INNER + c * SSM_NCH:D_INNER + (c + 1) * SSM_NCH],
                    preferred_element_type=F32)
        prev8 = carry_ref[:, cs]
        acc = cw_ref[SSM_CONV - 1:SSM_CONV, cs] * y + cb_ref[:, cs]
        for k in range(1, SSM_CONV):
            acc = acc + cw_ref[SSM_CONV - 1 - k:SSM_CONV - k, cs] * _shift_rows(y, prev8, k)
        carry_ref[:, cs] = y[y.shape[0] - SUBLANES:]
        ab = acc.astype(BF16)
        xbc_ref[:, cs] = ab * _sigmoid(ab)
    raw = jnp.dot(h, w_ref[:, D_INNER + XBC_W:], preferred_element_type=F32) + dtb_ref[...]
    dt_ref[...] = jnp.maximum(raw, 0.0) + jnp.log1p(jnp.exp(-jnp.abs(raw)))


def _ssm_inproj(x2, g, w, cw, cb, dtb, batch, seq):
    t = x2.shape[0]
    nt = seq // TM_PROJ
    row = lambda b, j: (b * nt + j, 0)
    return pl.pallas_call(
        _ssm_inproj_kernel,
        grid=(batch, nt),
        in_specs=[
            pl.BlockSpec((TM_PROJ, D_MODEL), row),
            _const_spec((1, D_MODEL)),
            _const_spec((D_MODEL, SSM_PROJ_W)),
            _const_spec((SSM_CONV, XBC_W)),
            _const_spec((1, XBC_W)),
            _const_spec((1, LANES)),
        ],
        out_specs=[
            pl.BlockSpec((TM_PROJ, D_INNER), row),
            pl.BlockSpec((TM_PROJ, XBC_W), row),
            pl.BlockSpec((TM_PROJ, LANES), row),
        ],
        out_shape=[
            jax.ShapeDtypeStruct((t, D_INNER), BF16),
            jax.ShapeDtypeStruct((t, XBC_W), BF16),
            jax.ShapeDtypeStruct((t, LANES), F32),
        ],
        scratch_shapes=[pltpu.VMEM((SUBLANES, XBC_W), F32)],
        compiler_params=_params(("parallel", "arbitrary")),
        name="ssm_inproj",
    )(x2, g, w, cw, cb, dtb)


def _cumsum_rows(a):
    n = a.shape[0]
    row = lax.broadcasted_iota(jnp.int32, a.shape, 0)
    d = 1
    while d < n:
        if d < SUBLANES:
            shifted = jnp.where(row < d, 0.0, pltpu.roll(a, d, axis=0))
        else:
            shifted = jnp.concatenate([jnp.zeros((d, a.shape[1]), F32), a[:n - d]], axis=0)
        a = a + shifted
        d *= 2
    return a


HEADS_PER_DOT = 4


def _ssd_chunk(r0, states, xbc_ref, dt_ref, alog_ref, dskip_ref, exp_ref, o_ref):
    ll = SSD_L
    rows = slice(r0, r0 + ll)
    nt = (((1,), (1,)), ((), ()))
    dt = dt_ref[rows, :]
    acs = _cumsum_rows(dt * -jnp.exp(alog_ref[...])) * LOG2E
    acs_t = jnp.transpose(acs)
    dt_t = jnp.transpose(dt)
    last = acs[ll - 1:ll, :]
    ea = jnp.exp2(acs)

    lane = lax.broadcasted_iota(jnp.int32, (1, LANES), 1)
    small = jnp.concatenate([jnp.exp2(last - acs) * dt, jnp.broadcast_to(ea[ll - 1:ll, :], (SUBLANES, LANES))],
                            axis=0)
    small = jnp.where(lane < SSM_HEADS, small, 0.0)
    hi = small.astype(BF16).astype(F32)
    r1 = small - hi
    mid = r1.astype(BF16).astype(F32)
    lo = (r1 - mid).astype(BF16).astype(F32)
    packed = (hi + pltpu.roll(mid, SSM_HEADS, axis=1) + pltpu.roll(lo, 2 * SSM_HEADS, axis=1)).astype(BF16)
    big = jnp.dot(packed, exp_ref[...], preferred_element_type=F32)
    w_e = big[:ll]
    decay_e = big[ll:ll + 1]

    causal = (lax.broadcasted_iota(jnp.int32, (ll, ll), 0) >= lax.broadcasted_iota(jnp.int32, (ll, ll), 1))
    wd = HEADS_PER_DOT * HEAD_DIM
    lane_w = lax.broadcasted_iota(jnp.int32, (1, wd), 1)
    new_states = []
    for g in range(SSM_GROUPS):
        c0 = g * GROUP_W
        state = states[g]
        state_b = state.astype(BF16)
        bmat = xbc_ref[rows, D_INNER + g * SSM_STATE:D_INNER + (g + 1) * SSM_STATE]
        c1 = D_INNER + (SSM_GROUPS + g) * SSM_STATE
        cmat = xbc_ref[rows, c1:c1 + SSM_STATE]
        cb = jnp.where(causal, lax.dot_general(cmat, bmat, nt, preferred_element_type=F32), 0.0)
        cf = cmat.astype(F32)
        ys = []
        for part in range(HEADS_PER_GROUP // HEADS_PER_DOT):
            lhs = []
            for hh in range(HEADS_PER_DOT):
                hd = g * HEADS_PER_GROUP + part * HEADS_PER_DOT + hh
                seg = jnp.minimum(acs[:, hd:hd + 1] - acs_t[hd:hd + 1, :], 0.0)
                m = cb * jnp.exp2(seg) * dt_t[hd:hd + 1, :]
                cea = cf * ea[:, hd:hd + 1]
                lhs.append(jnp.concatenate([m.astype(BF16), cea.astype(BF16)], axis=1))
            rhs = jnp.concatenate([xbc_ref[rows, c0 + part * wd:c0 + (part + 1) * wd],
                                   state_b[:, part * wd:(part + 1) * wd]], axis=0)
            out = jnp.dot(jnp.concatenate(lhs, axis=0), rhs, preferred_element_type=F32)
            y = out[(HEADS_PER_DOT - 1) * ll:]
            for hh in range(HEADS_PER_DOT - 2, -1, -1):
                y = jnp.where(lane_w < (hh + 1) * HEAD_DIM, out[hh * ll:(hh + 1) * ll], y)
            ys.append(y)
        xf = xbc_ref[rows, c0:c0 + GROUP_W].astype(F32)
        y = jnp.concatenate(ys, axis=1) + dskip_ref[:, c0:c0 + GROUP_W] * xf
        o_ref[rows, c0:c0 + GROUP_W] = y.astype(BF16)
        xw = (xf * w_e[:, c0:c0 + GROUP_W]).astype(BF16)
        new_states.append(state * decay_e[:, c0:c0 + GROUP_W] + lax.dot_general(
            bmat, xw, (((0,), (0,)), ((), ())), preferred_element_type=F32))
    return new_states


def _ssd_kernel(xbc_ref, dt_ref, alog_ref, dskip_ref, exp_ref, o_ref, state_ref):
    @pl.when(pl.program_id(1) == 0)
    def _():
        state_ref[...] = jnp.zeros_like(state_ref)

    states = [state_ref[:, g * GROUP_W:(g + 1) * GROUP_W] for g in range(SSM_GROUPS)]
    for c in range(SSD_ROWS // SSD_L):
        states = _ssd_chunk(c * SSD_L, states, xbc_ref, dt_ref, alog_ref, dskip_ref, exp_ref, o_ref)
    for g in range(SSM_GROUPS):
        state_ref[:, g * GROUP_W:(g + 1) * GROUP_W] = states[g]


def _ssd(xbc, dt, alog, dskip, expand, batch, seq):
    t = xbc.shape[0]
    nc = seq // SSD_ROWS
    row = lambda b, c: (b * nc + c, 0)
    return pl.pallas_call(
        _ssd_kernel,
        grid=(batch, nc),
        in_specs=[
            pl.BlockSpec((SSD_ROWS, XBC_W), row),
            pl.BlockSpec((SSD_ROWS, LANES), row),
            _const_spec((1, LANES)),
            _const_spec((1, D_INNER)),
            _const_spec((LANES, D_INNER)),
        ],
        out_specs=pl.BlockSpec((SSD_ROWS, D_INNER), row),
        out_shape=jax.ShapeDtypeStruct((t, D_INNER), BF16),
        scratch_shapes=[pltpu.VMEM((SSM_STATE, D_INNER), F32)],
        compiler_params=_params(("parallel", "arbitrary")),
        name="ssd",
    )(xbc, dt, alog, dskip, expand)


def _attn_weights(w_in, w_out, q_norm_a, k_norm_a, q_norm_b, k_norm_b):
    da = H_A * HEAD_DIM
    qa, ka, va, qb, kb, vb = jnp.split(w_in, [da, 2 * da, 3 * da, 4 * da, 4 * da + 128], axis=1)
    perm = np.concatenate([np.arange(HEAD_DIM) + (j + 4 * e) * HEAD_DIM
                           for j in range(H_B // 2) for e in range(2)])
    w = jnp.concatenate([qa, ka, qb[:, perm], va, kb, vb], axis=1).astype(BF16)
    scale = HEAD_DIM ** -0.5 * LOG2E
    ones = jnp.ones((1,), F32)
    gain = jnp.concatenate([
        jnp.tile(q_norm_a, H_A) * scale, jnp.tile(k_norm_a, H_A), jnp.tile(q_norm_b, H_B) * scale,
        jnp.tile(ones, da), jnp.tile(k_norm_b, H_B_KV), jnp.tile(ones, 128)]).reshape(1, QKV_W)
    wo = jnp.concatenate([w_out[:da], w_out[da:][perm]], axis=0).astype(BF16)
    return w, gain.astype(F32), wo


def _attn_bias(relpos_table):
    neg = -jnp.inf
    r = np.arange(SUBQ)[:, None]
    jj = np.arange(A_SUB_BAND)[None, :]
    dchunk = jj // CHUNK - r // CHUNK
    ok = (dchunk >= 0) & (dchunk <= A_PREV)
    tbl = relpos_table.astype(F32)
    n_clip = A_SUB_BAND - 1 - MAX_REL
    by_rel = jnp.concatenate(
        [tbl[:, MAX_REL - (SUBQ - 1):2 * MAX_REL],
         jnp.broadcast_to(tbl[:, 2 * MAX_REL:], (H_A, n_clip + 1))], axis=1)
    p = A_SUB_BAND + SUBQ
    rev = jnp.pad(by_rel[:, ::-1], ((0, 0), (0, 1)))
    toep = jnp.tile(rev, (1, SUBQ))[:, :SUBQ * (p - 1)].reshape(H_A, SUBQ, p - 1)
    toep = toep[:, :, SUBQ - 1:SUBQ - 1 + A_SUB_BAND]
    pad_ok = np.stack([jj >= A_PREV * CHUNK - n * SUBQ for n in range(A_VARIANTS)])
    bias_a = jnp.where((ok[None] & pad_ok)[:, None], toep[None] * LOG2E, neg)
    jb = np.arange(B_SUB_BAND)[None, :]
    relb = r - jb + B_PREV * CHUNK
    dcb = jb // CHUNK - r // CHUNK
    okb = (dcb >= 0) & (dcb <= B_PREV)
    slopes = 2.0 ** (-8.0 * jnp.arange(1, H_B + 1, dtype=F32) / H_B)
    alibi = -slopes[:, None, None] * jnp.abs(relb).astype(F32)[None] * LOG2E
    pad_okb = np.stack([jb >= B_PREV * CHUNK - n * SUBQ for n in range(B_VARIANTS)])
    bias_b = jnp.where((okb[None] & pad_okb)[:, None], alibi[None], neg)
    return bias_a, bias_b


def _block_diag_ones():
    i = np.arange(256)
    return jnp.asarray((i[:, None] // HEAD_DIM) == (i[None, :] // HEAD_DIM), dtype=BF16)


def _ssd_expand_matrix():
    e = np.zeros((LANES, D_INNER), np.float32)
    for part in range(3):
        for hd in range(SSM_HEADS):
            e[part * SSM_HEADS + hd, hd * HEAD_DIM:(hd + 1) * HEAD_DIM] = 1.0
    return jnp.asarray(e, dtype=BF16)


def kernel(x, norm_mix, norm_ffn, attn_w_in, attn_w_out, relpos_table, q_norm_a, k_norm_a, q_norm_b,
           k_norm_b, sinks, ssm_w_in, ssm_conv_w, ssm_conv_b, ssm_dt_bias, ssm_a_log, ssm_d, ssm_norm,
           ssm_w_out, ffn_w_in, ffn_conv_w, ffn_conv_b, ffn_w_out):
    batch, seq, _ = x.shape
    assert seq % TM_PROJ == 0 and seq % QBLK == 0 and seq % SSD_ROWS == 0 and seq % TM_FFN == 0
    x2 = x.reshape(batch * seq, D_MODEL)
    row = lambda v: v.reshape(1, -1).astype(F32)

    def ffn(layer, mix, xin, wo, gate=None):
        return _mix_ffn(mix, xin, wo, row(norm_ffn[layer]), ffn_w_in[layer].astype(BF16),
                        ffn_conv_w[layer].astype(F32), row(ffn_conv_b[layer]),
                        ffn_w_out[layer].astype(BF16), batch, seq, gate=gate)

    w, gain, wo = _attn_weights(attn_w_in[0], attn_w_out[0], q_norm_a[0], k_norm_a[0], q_norm_b[0],
                                k_norm_b[0])
    qkv = _attn_inproj(x2, row(norm_mix[0]), w, gain, _block_diag_ones())
    bias_a, bias_b = _attn_bias(relpos_table[0])
    heads = _attention(qkv, sinks[0].astype(F32) * LOG2E, bias_a, bias_b, batch, seq)
    x2 = ffn(0, heads, x2, wo)

    pad = SSM_PROJ_W - ssm_w_in.shape[2]
    w_ssm = jnp.pad(ssm_w_in[0], ((0, 0), (0, pad))).astype(BF16)
    pad_h = LANES - SSM_HEADS
    dtb = jnp.pad(ssm_dt_bias[0], (0, pad_h)).reshape(1, LANES).astype(F32)
    alog = jnp.pad(ssm_a_log[0], (0, pad_h)).reshape(1, LANES).astype(F32)
    z, xbc, dt = _ssm_inproj(x2, row(norm_mix[1]), w_ssm, ssm_conv_w[0].astype(F32),
                             row(ssm_conv_b[0]), dtb, batch, seq)
    dskip = row(jnp.repeat(ssm_d[0], HEAD_DIM))
    y = _ssd(xbc, dt, alog, dskip, _ssd_expand_matrix(), batch, seq)
    x2 = ffn(1, y, x2, ssm_w_out[0].astype(BF16), gate=(z, row(ssm_norm[0])))
    return x2.reshape(batch, seq, D_MODEL)
```

```python
import functools

import jax
import jax.numpy as jnp
import numpy as np
from jax import lax
from jax.experimental import pallas as pl
from jax.experimental.pallas import tpu as pltpu

F32 = jnp.float32
BF16 = jnp.bfloat16

LANES = 128
SUBLANES = 8
VMEM_LIMIT_BYTES = 56 * 1024 * 1024

D_MODEL = 1024
EPS = 1e-6
CHUNK = 64
HEAD_DIM = 64
H_A = 8
A_PREV = 8
MAX_REL = 256
H_B = 8
H_B_KV = 2
B_PREV = 2
D_INNER = 2048
SSM_HEADS = 32
SSM_GROUPS = 4
SSM_STATE = 128
SSM_CONV = 4
GROUP_W = D_INNER // SSM_GROUPS
HEADS_PER_GROUP = SSM_HEADS // SSM_GROUPS
D_FF = 2816
FFN_CONV = 3

QBLK = 512
A_KBLKS = -(-A_PREV * CHUNK // QBLK) + 1
PV_HEADS = 4
LOG2E = 1.4426950408889634
SUBQ = 128
A_SUB_BAND = SUBQ + A_PREV * CHUNK
B_SUB_BAND = SUBQ + B_PREV * CHUNK
A_VARIANTS = A_PREV * CHUNK // SUBQ + 1
B_VARIANTS = B_PREV * CHUNK // SUBQ + 1

QA0, KA0, QB0, VA0, KB0, VB0 = 0, 512, 1024, 1536, 2048, 2176
QKV_W = 2304
NORM_CHUNKS = ((0, 256), (256, 256), (512, 256), (768, 256), (1024, 256), (1280, 256), (KB0, 128))
COPY_CHUNKS = ((VA0, 512), (VB0, 128))

XBC_W = D_INNER + 2 * SSM_GROUPS * SSM_STATE
SSM_PROJ_W = D_INNER + XBC_W + LANES
SSD_L = 128
SSD_ROWS = 512

TM_PROJ = 1024
TM_FFN = 512
FFN_ROWS = 256


def _const_spec(shape):
    zeros = (0,) * len(shape)
    return pl.BlockSpec(shape, lambda *_: zeros, pipeline_mode=pl.Buffered(1))


def _params(sem):
    return pltpu.CompilerParams(dimension_semantics=sem, vmem_limit_bytes=VMEM_LIMIT_BYTES)


def _rms_rows(xf, gain):
    ms = jnp.mean(xf * xf, axis=-1, keepdims=True)
    return xf * lax.rsqrt(ms + EPS) * gain


def _shift_rows(y, prev8, k):
    r = pltpu.roll(y, k, axis=0)
    row = lax.broadcasted_iota(jnp.int32, (SUBLANES, 1), 0)
    first = jnp.where(row < k, pltpu.roll(prev8, k, axis=0), r[:SUBLANES])
    return jnp.concatenate([first, r[SUBLANES:]], axis=0)


def _sigmoid(x):
    return 1.0 / (1.0 + jnp.exp(-x))


def _attn_inproj_kernel(x_ref, g_ref, w_ref, gain_ref, bd_ref, o_ref):
    h = _rms_rows(x_ref[...], g_ref[...]).astype(BF16)
    y = jnp.dot(h, w_ref[...], preferred_element_type=F32)
    for c0, w in NORM_CHUNKS:
        yc = y[:, c0:c0 + w]
        ss = jnp.dot((yc * yc).astype(BF16), bd_ref[:w, :w], preferred_element_type=F32)
        r = lax.rsqrt(ss * (1.0 / HEAD_DIM) + EPS)
        o_ref[:, c0:c0 + w] = (yc * r * gain_ref[:, c0:c0 + w]).astype(BF16)
    for c0, w in COPY_CHUNKS:
        o_ref[:, c0:c0 + w] = y[:, c0:c0 + w].astype(BF16)


def _attn_inproj(x2, g, w, gain, bd):
    t = x2.shape[0]
    return pl.pallas_call(
        _attn_inproj_kernel,
        grid=(t // TM_PROJ,),
        in_specs=[
            pl.BlockSpec((TM_PROJ, D_MODEL), lambda i: (i, 0)),
            _const_spec((1, D_MODEL)),
            _const_spec((D_MODEL, QKV_W)),
            _const_spec((1, QKV_W)),
            _const_spec((256, 256)),
        ],
        out_specs=pl.BlockSpec((TM_PROJ, QKV_W), lambda i: (i, 0)),
        out_shape=jax.ShapeDtypeStruct((t, QKV_W), BF16),
        compiler_params=_params(("parallel",)),
        name="attn_inproj",
    )(x2, g, w, gain, bd)


def _attention_kernel(*refs):
    n_sub = QBLK // SUBQ
    sink_ref, qa_ref = refs[0], refs[1]
    ka_refs = refs[2:2 + A_KBLKS]
    va_refs = refs[2 + A_KBLKS:2 + 2 * A_KBLKS]
    qb_ref, kb1_ref, kb0_ref, vb1_ref, vb0_ref = refs[2 + 2 * A_KBLKS:7 + 2 * A_KBLKS]
    ba_refs = refs[7 + 2 * A_KBLKS:7 + 2 * A_KBLKS + n_sub]
    bb_refs = refs[7 + 2 * A_KBLKS + n_sub:7 + 2 * A_KBLKS + 2 * n_sub]
    o_ref = refs[-1]
    lane = lax.broadcasted_iota(jnp.int32, (1, LANES), 1)
    lo_half = lane < HEAD_DIM
    half_masks = (lo_half.astype(BF16), (~lo_half).astype(BF16))
    nt = (((1,), (1,)), ((), ()))
    quad_w = PV_HEADS * HEAD_DIM
    lane_q = lax.broadcasted_iota(jnp.int32, (1, quad_w), 1)
    ob0 = H_A * HEAD_DIM

    def two_heads(q):
        return jnp.concatenate([q * half_masks[0], q * half_masks[1]], axis=0)

    for sub in range(QBLK // SUBQ):
        rs = slice(sub * SUBQ, (sub + 1) * SUBQ)
        biasa_ref = ba_refs[sub]
        biasb_ref = bb_refs[sub]

        a0 = (A_KBLKS - 1) * QBLK + sub * SUBQ - A_PREV * CHUNK
        ka = slice(a0, a0 + A_SUB_BAND)
        for quad in range(H_A // PV_HEADS):
            ps, inv_l = [], []
            for pr in range(PV_HEADS // 2):
                sl = slice(quad * quad_w + pr * LANES, quad * quad_w + (pr + 1) * LANES)
                k = jnp.concatenate([r[:, sl] for r in ka_refs], axis=0)[ka]
                s2 = lax.dot_general(two_heads(qa_ref[rs, sl]), k, nt, preferred_element_type=F32)
                for e in range(2):
                    s = s2[e * SUBQ:(e + 1) * SUBQ] + biasa_ref[quad * PV_HEADS + 2 * pr + e]
                    p = jnp.exp2(s - jnp.max(s, axis=-1, keepdims=True))
                    inv_l.append(1.0 / jnp.sum(p, axis=-1, keepdims=True))
                    ps.append(p.astype(BF16))
            ql = slice(quad * quad_w, (quad + 1) * quad_w)
            v = jnp.concatenate([r[:, ql] for r in va_refs], axis=0)[ka]
            out = jnp.dot(jnp.concatenate(ps, axis=0), v, preferred_element_type=F32)
            o = out[(PV_HEADS - 1) * SUBQ:] * inv_l[PV_HEADS - 1]
            for hh in range(PV_HEADS - 2, -1, -1):
                o = jnp.where(lane_q < (hh + 1) * HEAD_DIM, out[hh * SUBQ:(hh + 1) * SUBQ] * inv_l[hh], o)
            o_ref[rs, ql] = o.astype(BF16)

        if sub == 0:
            kb = jnp.concatenate([kb1_ref[QBLK - B_PREV * CHUNK:, :], kb0_ref[:SUBQ, :]], axis=0)
            vb = jnp.concatenate([vb1_ref[QBLK - B_PREV * CHUNK:, :], vb0_ref[:SUBQ, :]], axis=0)
        else:
            kb = kb0_ref[sub * SUBQ - B_PREV * CHUNK:(sub + 1) * SUBQ, :]
            vb = vb0_ref[sub * SUBQ - B_PREV * CHUNK:(sub + 1) * SUBQ, :]
        ps, inv_l = [], []
        for j in range(H_B // 2):
            s2 = lax.dot_general(two_heads(qb_ref[rs, j * LANES:(j + 1) * LANES]), kb, nt,
                                 preferred_element_type=F32)
            for e in range(2):
                hd = j + (H_B // 2) * e
                snk = sink_ref[hd]
                s = s2[e * SUBQ:(e + 1) * SUBQ] + biasb_ref[hd]
                m = jnp.maximum(jnp.max(s, axis=-1, keepdims=True), snk)
                p = jnp.exp2(s - m)
                inv_l.append(1.0 / (jnp.sum(p, axis=-1, keepdims=True) + jnp.exp2(snk - m)))
                ps.append(p.astype(BF16))
        out = jnp.dot(jnp.concatenate(ps, axis=0), vb, preferred_element_type=F32)
        for j in range(H_B // 2):
            o0 = out[(2 * j) * SUBQ:(2 * j + 1) * SUBQ] * inv_l[2 * j]
            o1 = out[(2 * j + 1) * SUBQ:(2 * j + 2) * SUBQ] * inv_l[2 * j + 1]
            o_ref[rs, ob0 + j * LANES:ob0 + (j + 1) * LANES] = jnp.where(lo_half, o0, o1).astype(BF16)


def _attention(qkv, sinks, bias_a, bias_b, batch, seq):
    t = qkv.shape[0]
    nq = seq // QBLK

    def rows(d):
        return lambda b, i: b * nq + jnp.maximum(i - d, 0)

    def spec(width, col, d):
        r = rows(d)
        return pl.BlockSpec((QBLK, width), lambda b, i: (r(b, i), col))

    wa, wb = H_A * HEAD_DIM, H_B_KV * HEAD_DIM
    in_specs = [
        pl.BlockSpec(memory_space=pltpu.SMEM),
        spec(wa, QA0 // wa, 0),
        *[spec(wa, KA0 // wa, d) for d in range(A_KBLKS - 1, -1, -1)],
        *[spec(wa, VA0 // wa, d) for d in range(A_KBLKS - 1, -1, -1)],
        spec(wa, QB0 // wa, 0),
        spec(wb, KB0 // wb, 1), spec(wb, KB0 // wb, 0),
        spec(wb, VB0 // wb, 1), spec(wb, VB0 // wb, 0),
    ]
    n_sub = QBLK // SUBQ
    for shape, n_var in (((None, H_A, SUBQ, A_SUB_BAND), A_VARIANTS), ((None, H_B, SUBQ, B_SUB_BAND), B_VARIANTS)):
        for sub in range(n_sub):
            in_specs.append(pl.BlockSpec(
                shape, lambda b, i, sub=sub, n_var=n_var: (jnp.minimum(n_sub * i + sub, n_var - 1), 0, 0, 0)))
    return pl.pallas_call(
        _attention_kernel,
        grid=(batch, nq),
        in_specs=in_specs,
        out_specs=pl.BlockSpec((QBLK, D_MODEL), lambda b, i: (b * nq + i, 0)),
        out_shape=jax.ShapeDtypeStruct((t, D_MODEL), BF16),
        compiler_params=_params(("parallel", "parallel")),
        name="attention",
    )(sinks, *([qkv] * (6 + 2 * A_KBLKS)), *([bias_a] * n_sub), *([bias_b] * n_sub))


def _gate_norm(y, z, nw_ref):
    gated = y.astype(F32) * (z * _sigmoid(z)).astype(F32)
    outs = []
    for g in range(SSM_GROUPS):
        cs = slice(g * GROUP_W, (g + 1) * GROUP_W)
        outs.append(_rms_rows(gated[:, cs], nw_ref[:, cs]).astype(BF16))
    return jnp.concatenate(outs, axis=1)


def _mix_ffn_kernel(*refs, gated):
    if gated:
        mix_ref, z_ref, nw_ref, x_ref, wo_ref, gn_ref, win_ref, cw_ref, cb_ref, wd_ref, o_ref, carry_ref = refs
    else:
        mix_ref, x_ref, wo_ref, gn_ref, win_ref, cw_ref, cb_ref, wd_ref, o_ref, carry_ref = refs

    @pl.when(pl.program_id(1) == 0)
    def _():
        carry_ref[...] = jnp.zeros_like(carry_ref)

    prev8 = carry_ref[...]
    for r in range(TM_FFN // FFN_ROWS):
        rs = slice(r * FFN_ROWS, (r + 1) * FFN_ROWS)
        mix = mix_ref[rs, :]
        if gated:
            mix = _gate_norm(mix, z_ref[rs, :], nw_ref)
        x1 = x_ref[rs, :] + jnp.dot(mix, wo_ref[...], preferred_element_type=F32)
        h = _rms_rows(x1, gn_ref[...]).astype(BF16)
        g = jnp.dot(h, win_ref[:, :D_FF], preferred_element_type=F32)
        u = jnp.dot(h, win_ref[:, D_FF:], preferred_element_type=F32)
        gc = (cw_ref[0:1, :] * _shift_rows(g, prev8, 2) + cw_ref[1:2, :] * _shift_rows(g, prev8, 1)
              + cw_ref[2:3, :] * g + cb_ref[...])
        prev8 = g[FFN_ROWS - SUBLANES:]
        act = (gc * _sigmoid(gc) * u).astype(BF16)
        o_ref[rs, :] = x1 + jnp.dot(act, wd_ref[...], preferred_element_type=F32)
    carry_ref[...] = prev8


def _mix_ffn(mix, x2, wo, gn, w_in, cw, cb, wd, batch, seq, gate=None):
    t, kmix = mix.shape
    nt = seq // TM_FFN
    row = lambda b, j: (b * nt + j, 0)
    gate_specs = [] if gate is None else [pl.BlockSpec((TM_FFN, kmix), row), _const_spec((1, kmix))]
    return pl.pallas_call(
        functools.partial(_mix_ffn_kernel, gated=gate is not None),
        grid=(batch, nt),
        in_specs=[
            pl.BlockSpec((TM_FFN, kmix), row),
            *gate_specs,
            pl.BlockSpec((TM_FFN, D_MODEL), row),
            _const_spec((kmix, D_MODEL)),
            _const_spec((1, D_MODEL)),
            _const_spec((D_MODEL, 2 * D_FF)),
            _const_spec((FFN_CONV, D_FF)),
            _const_spec((1, D_FF)),
            _const_spec((D_FF, D_MODEL)),
        ],
        out_specs=pl.BlockSpec((TM_FFN, D_MODEL), row),
        out_shape=jax.ShapeDtypeStruct((t, D_MODEL), F32),
        scratch_shapes=[pltpu.VMEM((SUBLANES, D_FF), F32)],
        compiler_params=_params(("parallel", "arbitrary")),
        name="mix_ffn",
    )(mix, *(() if gate is None else gate), x2, wo, gn, w_in, cw, cb, wd)


SSM_NCH = 512


def _ssm_inproj_kernel(x_ref, g_ref, w_ref, cw_ref, cb_ref, dtb_ref, z_ref, xbc_ref, dt_ref,
                       carry_ref):
    @pl.when(pl.program_id(1) == 0)
    def _():
        carry_ref[...] = jnp.zeros_like(carry_ref)

    h = _rms_rows(x_ref[...], g_ref[...]).astype(BF16)
    n_z = D_INNER // SSM_NCH
    for c in range(XBC_W // SSM_NCH):
        if c < n_z:
            cs = slice(c * SSM_NCH, (c + 1) * SSM_NCH)
            z_ref[:, cs] = jnp.dot(h, w_ref[:, cs], preferred_element_type=F32).astype(BF16)
        cs = slice(c * SSM_NCH, (c + 1) * SSM_NCH)
        y = jnp.dot(h, w_ref[:, D_INNER + c * SSM_NCH:D_INNER + (c + 1) * SSM_NCH],
                    preferred_element_type=F32)
        prev8 = carry_ref[:, cs]
        acc = cw_ref[SSM_CONV - 1:SSM_CONV, cs] * y + cb_ref[:, cs]
        for k in range(1, SSM_CONV):
            acc = acc + cw_ref[SSM_CONV - 1 - k:SSM_CONV - k, cs] * _shift_rows(y, prev8, k)
        carry_ref[:, cs] = y[y.shape[0] - SUBLANES:]
        ab = acc.astype(BF16)
        xbc_ref[:, cs] = ab * _sigmoid(ab)
    raw = jnp.dot(h, w_ref[:, D_INNER + XBC_W:], preferred_element_type=F32) + dtb_ref[...]
    dt_ref[...] = jnp.maximum(raw, 0.0) + jnp.log1p(jnp.exp(-jnp.abs(raw)))


def _ssm_inproj(x2, g, w, cw, cb, dtb, batch, seq):
    t = x2.shape[0]
    nt = seq // TM_PROJ
    row = lambda b, j: (b * nt + j, 0)
    return pl.pallas_call(
        _ssm_inproj_kernel,
        grid=(batch, nt),
        in_specs=[
            pl.BlockSpec((TM_PROJ, D_MODEL), row),
            _const_spec((1, D_MODEL)),
            _const_spec((D_MODEL, SSM_PROJ_W)),
            _const_spec((SSM_CONV, XBC_W)),
            _const_spec((1, XBC_W)),
            _const_spec((1, LANES)),
        ],
        out_specs=[
            pl.BlockSpec((TM_PROJ, D_INNER), row),
            pl.BlockSpec((TM_PROJ, XBC_W), row),
            pl.BlockSpec((TM_PROJ, LANES), row),
        ],
        out_shape=[
            jax.ShapeDtypeStruct((t, D_INNER), BF16),
            jax.ShapeDtypeStruct((t, XBC_W), BF16),
            jax.ShapeDtypeStruct((t, LANES), F32),
        ],
        scratch_shapes=[pltpu.VMEM((SUBLANES, XBC_W), F32)],
        compiler_params=_params(("parallel", "arbitrary")),
        name="ssm_inproj",
    )(x2, g, w, cw, cb, dtb)


def _cumsum_rows(a):
    n = a.shape[0]
    row = lax.broadcasted_iota(jnp.int32, a.shape, 0)
    d = 1
    while d < n:
        if d < SUBLANES:
            shifted = jnp.where(row < d, 0.0, pltpu.roll(a, d, axis=0))
        else:
            shifted = jnp.concatenate([jnp.zeros((d, a.shape[1]), F32), a[:n - d]], axis=0)
        a = a + shifted
        d *= 2
    return a


HEADS_PER_DOT = 4


def _ssd_chunk(r0, states, xbc_ref, dt_ref, alog_ref, dskip_ref, exp_ref, o_ref):
    ll = SSD_L
    rows = slice(r0, r0 + ll)
    nt = (((1,), (1,)), ((), ()))
    dt = dt_ref[rows, :]
    acs = _cumsum_rows(dt * -jnp.exp(alog_ref[...])) * LOG2E
    acs_t = jnp.transpose(acs)
    dt_t = jnp.transpose(dt)
    last = acs[ll - 1:ll, :]
    ea = jnp.exp2(acs)

    lane = lax.broadcasted_iota(jnp.int32, (1, LANES), 1)
    small = jnp.concatenate([jnp.exp2(last - acs) * dt, jnp.broadcast_to(ea[ll - 1:ll, :], (SUBLANES, LANES))],
                            axis=0)
    small = jnp.where(lane < SSM_HEADS, small, 0.0)
    hi = small.astype(BF16).astype(F32)
    r1 = small - hi
    mid = r1.astype(BF16).astype(F32)
    lo = (r1 - mid).astype(BF16).astype(F32)
    packed = (hi + pltpu.roll(mid, SSM_HEADS, axis=1) + pltpu.roll(lo, 2 * SSM_HEADS, axis=1)).astype(BF16)
    big = jnp.dot(packed, exp_ref[...], preferred_element_type=F32)
    w_e = big[:ll]
    decay_e = big[ll:ll + 1]

    causal = (lax.broadcasted_iota(jnp.int32, (ll, ll), 0) >= lax.broadcasted_iota(jnp.int32, (ll, ll), 1))
    wd = HEADS_PER_DOT * HEAD_DIM
    lane_w = lax.broadcasted_iota(jnp.int32, (1, wd), 1)
    new_states = []
    for g in range(SSM_GROUPS):
        c0 = g * GROUP_W
        state = states[g]
        state_b = state.astype(BF16)
        bmat = xbc_ref[rows, D_INNER + g * SSM_STATE:D_INNER + (g + 1) * SSM_STATE]
        c1 = D_INNER + (SSM_GROUPS + g) * SSM_STATE
        cmat = xbc_ref[rows, c1:c1 + SSM_STATE]
        cb = jnp.where(causal, lax.dot_general(cmat, bmat, nt, preferred_element_type=F32), 0.0)
        cf = cmat.astype(F32)
        ys = []
        for part in range(HEADS_PER_GROUP // HEADS_PER_DOT):
            lhs = []
            for hh in range(HEADS_PER_DOT):
                hd = g * HEADS_PER_GROUP + part * HEADS_PER_DOT + hh
                seg = jnp.minimum(acs[:, hd:hd + 1] - acs_t[hd:hd + 1, :], 0.0)
                m = cb * jnp.exp2(seg) * dt_t[hd:hd + 1, :]
                cea = cf * ea[:, hd:hd + 1]
                lhs.append(jnp.concatenate([m.astype(BF16), cea.astype(BF16)], axis=1))
            rhs = jnp.concatenate([xbc_ref[rows, c0 + part * wd:c0 + (part + 1) * wd],
                                   state_b[:, part * wd:(part + 1) * wd]], axis=0)
            out = jnp.dot(jnp.concatenate(lhs, axis=0), rhs, preferred_element_type=F32)
            y = out[(HEADS_PER_DOT - 1) * ll:]
            for hh in range(HEADS_PER_DOT - 2, -1, -1):
                y = jnp.where(lane_w < (hh + 1) * HEAD_DIM, out[hh * ll:(hh + 1) * ll], y)
            ys.append(y)
        xf = xbc_ref[rows, c0:c0 + GROUP_W].astype(F32)
        y = jnp.concatenate(ys, axis=1) + dskip_ref[:, c0:c0 + GROUP_W] * xf
        o_ref[rows, c0:c0 + GROUP_W] = y.astype(BF16)
        xw = (xf * w_e[:, c0:c0 + GROUP_W]).astype(BF16)
        new_states.append(state * decay_e[:, c0:c0 + GROUP_W] + lax.dot_general(
            bmat, xw, (((0,), (0,)), ((), ())), preferred_element_type=F32))
    return new_states


def _ssd_kernel(xbc_ref, dt_ref, alog_ref, dskip_ref, exp_ref, o_ref, state_ref):
    @pl.when(pl.program_id(1) == 0)
    def _():
        state_ref[...] = jnp.zeros_like(state_ref)

    states = [state_ref[:, g * GROUP_W:(g + 1) * GROUP_W] for g in range(SSM_GROUPS)]
    for c in range(SSD_ROWS // SSD_L):
        states = _ssd_chunk(c * SSD_L, states, xbc_ref, dt_ref, alog_ref, dskip_ref, exp_ref, o_ref)
    for g in range(SSM_GROUPS):
        state_ref[:, g * GROUP_W:(g + 1) * GROUP_W] = states[g]


def _ssd(xbc, dt, alog, dskip, expand, batch, seq):
    t = xbc.shape[0]
    nc = seq // SSD_ROWS
    row = lambda b, c: (b * nc + c, 0)
    return pl.pallas_call(
        _ssd_kernel,
        grid=(batch, nc),
        in_specs=[
            pl.BlockSpec((SSD_ROWS, XBC_W), row),
            pl.BlockSpec((SSD_ROWS, LANES), row),
            _const_spec((1, LANES)),
            _const_spec((1, D_INNER)),
            _const_spec((LANES, D_INNER)),
        ],
        out_specs=pl.BlockSpec((SSD_ROWS, D_INNER), row),
        out_shape=jax.ShapeDtypeStruct((t, D_INNER), BF16),
        scratch_shapes=[pltpu.VMEM((SSM_STATE, D_INNER), F32)],
        compiler_params=_params(("parallel", "arbitrary")),
        name="ssd",
    )(xbc, dt, alog, dskip, expand)


def _attn_weights(w_in, w_out, q_norm_a, k_norm_a, q_norm_b, k_norm_b):
    da = H_A * HEAD_DIM
    qa, ka, va, qb, kb, vb = jnp.split(w_in, [da, 2 * da, 3 * da, 4 * da, 4 * da + 128], axis=1)
    perm = np.concatenate([np.arange(HEAD_DIM) + (j + 4 * e) * HEAD_DIM
                           for j in range(H_B // 2) for e in range(2)])
    w = jnp.concatenate([qa, ka, qb[:, perm], va, kb, vb], axis=1).astype(BF16)
    scale = HEAD_DIM ** -0.5 * LOG2E
    ones = jnp.ones((1,), F32)
    gain = jnp.concatenate([
        jnp.tile(q_norm_a, H_A) * scale, jnp.tile(k_norm_a, H_A), jnp.tile(q_norm_b, H_B) * scale,
        jnp.tile(ones, da), jnp.tile(k_norm_b, H_B_KV), jnp.tile(ones, 128)]).reshape(1, QKV_W)
    wo = jnp.concatenate([w_out[:da], w_out[da:][perm]], axis=0).astype(BF16)
    return w, gain.astype(F32), wo


def _attn_bias(relpos_table):
    neg = -jnp.inf
    r = np.arange(SUBQ)[:, None]
    jj = np.arange(A_SUB_BAND)[None, :]
    dchunk = jj // CHUNK - r // CHUNK
    ok = (dchunk >= 0) & (dchunk <= A_PREV)
    tbl = relpos_table.astype(F32)
    n_clip = A_SUB_BAND - 1 - MAX_REL
    by_rel = jnp.concatenate(
        [tbl[:, MAX_REL - (SUBQ - 1):2 * MAX_REL],
         jnp.broadcast_to(tbl[:, 2 * MAX_REL:], (H_A, n_clip + 1))], axis=1)
    p = A_SUB_BAND + SUBQ
    rev = jnp.pad(by_rel[:, ::-1], ((0, 0), (0, 1)))
    toep = jnp.tile(rev, (1, SUBQ))[:, :SUBQ * (p - 1)].reshape(H_A, SUBQ, p - 1)
    toep = toep[:, :, SUBQ - 1:SUBQ - 1 + A_SUB_BAND]
    pad_ok = np.stack([jj >= A_PREV * CHUNK - n * SUBQ for n in range(A_VARIANTS)])
    bias_a = jnp.where((ok[None] & pad_ok)[:, None], toep[None] * LOG2E, neg)
    jb = np.arange(B_SUB_BAND)[None, :]
    relb = r - jb + B_PREV * CHUNK
    dcb = jb // CHUNK - r // CHUNK
    okb = (dcb >= 0) & (dcb <= B_PREV)
    slopes = 2.0 ** (-8.0 * jnp.arange(1, H_B + 1, dtype=F32) / H_B)
    alibi = -slopes[:, None, None] * jnp.abs(relb).astype(F32)[None] * LOG2E
    pad_okb = np.stack([jb >= B_PREV * CHUNK - n * SUBQ for n in range(B_VARIANTS)])
    bias_b = jnp.where((okb[None] & pad_okb)[:, None], alibi[None], neg)
    return bias_a, bias_b


def _block_diag_ones():
    i = np.arange(256)
    return jnp.asarray((i[:, None] // HEAD_DIM) == (i[None, :] // HEAD_DIM), dtype=BF16)


def _ssd_expand_matrix():
    e = np.zeros((LANES, D_INNER), np.float32)
    for part in range(3):
        for hd in range(SSM_HEADS):
            e[part * SSM_HEADS + hd, hd * HEAD_DIM:(hd + 1) * HEAD_DIM] = 1.0
    return jnp.asarray(e, dtype=BF16)


def kernel(x, norm_mix, norm_ffn, attn_w_in, attn_w_out, relpos_table, q_norm_a, k_norm_a, q_norm_b,
           k_norm_b, sinks, ssm_w_in, ssm_conv_w, ssm_conv_b, ssm_dt_bias, ssm_a_log, ssm_d, ssm_norm,
           ssm_w_out, ffn_w_in, ffn_conv_w, ffn_conv_b, ffn_w_out):
    batch, seq, _ = x.shape
    assert seq % TM_PROJ == 0 and seq % QBLK == 0 and seq % SSD_ROWS == 0 and seq % TM_FFN == 0
    x2 = x.reshape(batch * seq, D_MODEL)
    row = lambda v: v.reshape(1, -1).astype(F32)

    def ffn(layer, mix, xin, wo, gate=None):
        return _mix_ffn(mix, xin, wo, row(norm_ffn[layer]), ffn_w_in[layer].astype(BF16),
                        ffn_conv_w[layer].astype(F32), row(ffn_conv_b[layer]),
                        ffn_w_out[layer].astype(BF16), batch, seq, gate=gate)

    w, gain, wo = _attn_weights(attn_w_in[0], attn_w_out[0], q_norm_a[0], k_norm_a[0], q_norm_b[0],
                                k_norm_b[0])
    qkv = _attn_inproj(x2, row(norm_mix[0]), w, gain, _block_diag_ones())
    bias_a, bias_b = _attn_bias(relpos_table[0])
    heads = _attention(qkv, sinks[0].astype(F32) * LOG2E, bias_a, bias_b, batch, seq)
    x2 = ffn(0, heads, x2, wo)

    pad = SSM_PROJ_W - ssm_w_in.shape[2]
    w_ssm = jnp.pad(ssm_w_in[0], ((0, 0), (0, pad))).astype(BF16)
    pad_h = LANES - SSM_HEADS
    dtb = jnp.pad(ssm_dt_bias[0], (0, pad_h)).reshape(1, LANES).astype(F32)
    alog = jnp.pad(ssm_a_log[0], (0, pad_h)).reshape(1, LANES).astype(F32)
    z, xbc, dt = _ssm_inproj(x2, row(norm_mix[1]), w_ssm, ssm_conv_w[0].astype(F32),
                             row(ssm_conv_b[0]), dtb, batch, seq)
    dskip = row(jnp.repeat(ssm_d[0], HEAD_DIM))
    y = _ssd(xbc, dt, alog, dskip, _ssd_expand_matrix(), batch, seq)
    x2 = ffn(1, y, x2, ssm_w_out[0].astype(BF16), gate=(z, row(ssm_norm[0])))
    return x2.reshape(batch, seq, D_MODEL)
```
